```python
import math
import jax
import jax.numpy as jnp
from jax import lax
import numpy as np

D_MODEL = 2048
BATCH = 4
SEQ = 2048
DEPTH = 1
DEC_BATCH = 128
DEC_SEQ = 4
PAST_LEN = 2048
PAGE_SIZE = 128

HEAD_DIM = 128
FOX_HEADS = 8
NSA_HEADS = 8
NSA_KV_HEADS = 2
NSA_GROUP = NSA_HEADS // NSA_KV_HEADS
CMP_STRIDE = 16
CMP_BLOCK = 2 * CMP_STRIDE
SEL_BLOCK = 64
N_SEL = 16
WINDOW = 512
MEM_TOKENS = 256
MEM_HEADS = 4
PEER_HEADS = 8
PEER_KEYS = 128
PEER_EXPERTS = PEER_KEYS * PEER_KEYS
PEER_TOPK = 16
PEER_QDIM = 256
Q_BLOCK = 128
PEER_BLOCK = 128
FORGET_BIAS = 4.0
FORCE_BONUS = 1.0e4
NEG = -1.0e30
EPS = 1e-6

FOX_W = FOX_HEADS * HEAD_DIM
NSA_W = NSA_HEADS * HEAD_DIM
KV_W = NSA_KV_HEADS * HEAD_DIM
MIX_W = FOX_W + NSA_W
MEM_W = MEM_HEADS * HEAD_DIM
IN_SIZES = (FOX_W, FOX_W, FOX_W, FOX_HEADS, NSA_W, KV_W, KV_W, KV_W, KV_W, KV_W, KV_W, 3 * NSA_HEADS)
P_IN = sum(IN_SIZES)

kernel_name = 'hymba_fox_nsa_peer_decoder_step'

F32 = jnp.float32


def rmsnorm(x, g):
    xf = x.astype(F32)
    y = xf * lax.rsqrt(jnp.mean(xf * xf, axis=-1, keepdims=True) + EPS)
    return (y * g.astype(F32)).astype(x.dtype)


def masked_softmax(s, mask):
    p = jax.nn.softmax(jnp.where(mask, s, NEG), axis=-1)
    return jnp.where(mask, p, 0.0)


def alibi_slopes(n):
    return 2.0 ** (-8.0 * jnp.arange(1, n + 1, dtype=F32) / n)


def sweep_query_blocks(fn, q_arrays, tq):
    T = tq.shape[0]
    nb = T // Q_BLOCK
    xs = tuple(jnp.swapaxes(a.reshape(a.shape[0], nb, Q_BLOCK, *a.shape[2:]), 0, 1) for a in q_arrays)
    out = lax.map(lambda args: fn(*args), xs + (tq.reshape(nb, Q_BLOCK),))
    out = jnp.swapaxes(out, 0, 1)
    return out.reshape(out.shape[0], T, *out.shape[3:])


def last_rows(a, n):
    T = a.shape[1]
    if T < n:
        a = jnp.pad(a, ((0, 0), (n - T, 0)) + ((0, 0),) * (a.ndim - 2))
    return a[:, -n:]


def project_mixers(x, g_mix, w_in, b_fox_f, b_nsa_gate):
    B, T = x.shape[:2]
    xn = rmsnorm(x, g_mix)
    offs = [int(o) for o in np.cumsum(IN_SIZES)[:-1]]
    fq, fk, fv, ff, nq, kc, vc, ks, vs, kw, vw, ng = jnp.split(xn @ w_in, offs, axis=-1)
    hd = lambda a: a.reshape(B, T, -1, HEAD_DIM)
    logf = jax.nn.log_sigmoid(ff.astype(F32) + b_fox_f.astype(F32))
    gates = jax.nn.sigmoid(ng.astype(F32) + b_nsa_gate.astype(F32)).reshape(B, T, NSA_HEADS, 3)
    return (hd(fq), hd(fk), hd(fv), logf, hd(nq), hd(kc), hd(vc), hd(ks), hd(vs), hd(kw), hd(vw), gates)


def fox_attend(q, cq, tq, k, v, ck, tk):
    s = jnp.einsum('bqhd,bkhd->bhqk', q, k).astype(F32) * HEAD_DIM ** -0.5
    s = s + jnp.swapaxes(cq, 1, 2)[..., :, None] - jnp.swapaxes(ck, 1, 2)[..., None, :]
    p = masked_softmax(s, tk[None, :] <= tq[:, None])
    return jnp.einsum('bhqk,bkhd->bqhd', p.astype(v.dtype), v)


def fox_prompt(fq, fk, fv, logf):
    B, S = fq.shape[:2]
    t = jnp.arange(S, dtype=jnp.int32)
    c = jnp.cumsum(logf, axis=1)
    o = sweep_query_blocks(lambda qb, cb, tb: fox_attend(qb, cb, tb, fk, fv, c, t), (fq, c), t)
    return o.reshape(B, S, FOX_W)


def compress(k, w_pos, w_phi):
    B, L, G, D = k.shape
    n_chunk = L // CMP_STRIDE
    c = k[:, :n_chunk * CMP_STRIDE].reshape(B, n_chunk, CMP_STRIDE, G, D)
    blocks = jnp.concatenate([c[:, :-1], c[:, 1:]], axis=2)
    pooled = jnp.einsum('bnlgd,ld->bngd', blocks, w_pos)
    return pooled @ w_phi


def cmp_positions(n_cmp):
    return jnp.arange(n_cmp, dtype=jnp.int32) * CMP_STRIDE + (CMP_BLOCK - 1)


def to_sel_blocks(k):
    B, L, G, D = k.shape
    n = -(-L // SEL_BLOCK)
    k = jnp.pad(k, ((0, 0), (0, n * SEL_BLOCK - L), (0, 0), (0, 0)))
    return k.reshape(B, n, SEL_BLOCK, G, D).transpose(0, 3, 1, 2, 4)


def nsa_attend(q, gates, tq, k_cmp, v_cmp, t_cmp, ks_blk, vs_blk, kw, vw, tw):
    B, T = q.shape[:2]
    G, HG = NSA_KV_HEADS, NSA_GROUP
    qg = q.reshape(B, T, G, HG, HEAD_DIM)
    slope = alibi_slopes(NSA_HEADS).reshape(G, HG)
    scale = HEAD_DIM ** -0.5

    s = jnp.einsum('btghd,bngd->btghn', qg, k_cmp).astype(F32) * scale
    dist = (tq[:, None] - t_cmp[None, :]).astype(F32)
    s = s - slope[None, None, :, :, None] * dist[None, :, None, None, :]
    p_cmp = masked_softmax(s, (t_cmp[None, :] <= tq[:, None])[None, :, None, None, :])
    o_cmp = jnp.einsum('btghn,bngd->btghd', p_cmp.astype(v_cmp.dtype), v_cmp)

    n_cmp, n_slc = k_cmp.shape[1], ks_blk.shape[2]
    jc = jnp.arange(n_cmp)[:, None] * CMP_STRIDE
    js = jnp.arange(n_slc)[None, :] * SEL_BLOCK
    sel_map = ((jc < js + SEL_BLOCK) & (jc + CMP_BLOCK > js)).astype(F32)
    imp = jnp.einsum('btghn,ns->btgs', p_cmp, sel_map)
    blk = jnp.arange(n_slc)[None, :]
    cur = (tq // SEL_BLOCK)[:, None]
    valid = (blk <= cur)[None, :, None, :]
    forced = ((blk == 0) | (blk == cur) | (blk == cur - 1))[None, :, None, :]
    score = jnp.where(valid, imp + jnp.where(forced, FORCE_BONUS, 0.0), NEG)
    top_s, idx = lax.top_k(score, min(N_SEL, n_slc))
    picked = top_s > 0.5 * NEG

    bi = jnp.arange(B)[:, None, None, None]
    gi = jnp.arange(G)[None, None, :, None]
    k_sel = ks_blk[bi, gi, idx]
    v_sel = vs_blk[bi, gi, idx]
    kpos = idx[..., None] * SEL_BLOCK + jnp.arange(SEL_BLOCK, dtype=jnp.int32)
    s = jnp.einsum('btghd,btgkld->btghkl', qg, k_sel).astype(F32) * scale
    dist = (tq[None, :, None, None, None] - kpos).astype(F32)
    s = s - slope[None, None, :, :, None, None] * dist[:, :, :, None]
    m = (picked[..., None] & (kpos <= tq[None, :, None, None, None]))[:, :, :, None]
    sf = s.reshape(B, T, G, HG, -1)
    p = masked_softmax(sf, jnp.broadcast_to(m, s.shape).reshape(sf.shape)).reshape(s.shape)
    o_sel = jnp.einsum('btghkl,btgkld->btghd', p.astype(v_sel.dtype), v_sel)

    s = jnp.einsum('btghd,bkgd->btghk', qg, kw).astype(F32) * scale
    dwin = tq[:, None] - tw[None, :]
    s = s - slope[None, None, :, :, None] * dwin.astype(F32)[None, :, None, None, :]
    mw = (dwin >= 0) & (dwin < WINDOW) & (tw[None, :] >= 0)
    p = masked_softmax(s, mw[None, :, None, None, :])
    o_win = jnp.einsum('btghk,bkgd->btghd', p.astype(vw.dtype), vw)

    g = gates.reshape(B, T, G, HG, 3)
    o = g[..., 0:1] * o_cmp + g[..., 1:2] * o_sel + g[..., 2:3] * o_win
    return o.reshape(B, T, NSA_W).astype(q.dtype)


def nsa_prompt(nq, gates, kc, vc, ks, vs, kw, vw, cmp_pos_k, cmp_phi_k, cmp_pos_v, cmp_phi_v):
    S = nq.shape[1]
    t_all = jnp.arange(S, dtype=jnp.int32)

    def one_seq(args):
        q1, g1, kc1, vc1, ks1, vs1, kw1, vw1 = [a[None] for a in args]
        k_cmp = compress(kc1, cmp_pos_k, cmp_phi_k)
        v_cmp = compress(vc1, cmp_pos_v, cmp_phi_v)
        t_cmp = cmp_positions(k_cmp.shape[1])
        ks_blk, vs_blk = to_sel_blocks(ks1), to_sel_blocks(vs1)
        pad = ((0, 0), (WINDOW, 0), (0, 0), (0, 0))
        kw_pad, vw_pad = jnp.pad(kw1, pad), jnp.pad(vw1, pad)

        def block(qb, gb, tb):
            start = tb[0]
            kwb = lax.dynamic_slice_in_dim(kw_pad, start, WINDOW + Q_BLOCK, axis=1)
            vwb = lax.dynamic_slice_in_dim(vw_pad, start, WINDOW + Q_BLOCK, axis=1)
            tw = start - WINDOW + jnp.arange(WINDOW + Q_BLOCK, dtype=jnp.int32)
            return nsa_attend(qb, gb, tb, k_cmp, v_cmp, t_cmp, ks_blk, vs_blk, kwb, vwb, tw)

        return sweep_query_blocks(block, (q1, g1), t_all)[0]

    return lax.map(one_seq, (nq, gates, kc, vc, ks, vs, kw, vw))


def mixers_sample(fq, fk, fv, lf, nq, gates, kc, vc, ks, vs, kw_all, vw_all, page_table,
                  fox_k_pool, fox_v_pool, fox_lf_pool, cmp_k_pool, cmp_v_pool, slc_k_pool, slc_v_pool,
                  cmp_pos_k, cmp_phi_k, cmp_pos_v, cmp_phi_v):
    T = fq.shape[1]
    L = PAST_LEN + T
    tq = PAST_LEN + jnp.arange(T, dtype=jnp.int32)
    tk = jnp.arange(L, dtype=jnp.int32)
    n_win = kw_all.shape[1]
    tw = L - n_win + jnp.arange(n_win, dtype=jnp.int32)

    def one_seq(args):
        fq1, fk1, fv1, lf1, nq1, g1, kc1, vc1, ks1, vs1, kw1, vw1, pages = args

        def full(pool, new):
            past = pool[pages].reshape(PAST_LEN, *pool.shape[2:])
            return jnp.concatenate([past.astype(new.dtype), new], axis=0)[None]

        c = jnp.cumsum(full(fox_lf_pool, lf1), axis=1)
        fox_o = fox_attend(fq1[None], c[:, PAST_LEN:], tq, full(fox_k_pool, fk1), full(fox_v_pool, fv1), c, tk)
        k_cmp = compress(full(cmp_k_pool, kc1), cmp_pos_k, cmp_phi_k)
        v_cmp = compress(full(cmp_v_pool, vc1), cmp_pos_v, cmp_phi_v)
        nsa_o = nsa_attend(nq1[None], g1[None], tq, k_cmp, v_cmp, cmp_positions(k_cmp.shape[1]),
                           to_sel_blocks(full(slc_k_pool, ks1)), to_sel_blocks(full(slc_v_pool, vs1)),
                           kw1[None], vw1[None], tw)
        return fox_o[0].reshape(T, FOX_W), nsa_o[0]

    return lax.map(one_seq, (fq, fk, fv, lf, nq, gates, kc, vc, ks, vs, kw_all, vw_all, page_table))


def merge_mixers(fox_o, nsa_o, g_fox_out, g_nsa_out, w_out):
    y = jnp.concatenate([rmsnorm(fox_o, g_fox_out), rmsnorm(nsa_o, g_nsa_out)], axis=-1)
    return y @ w_out


def memory_kv(mem, g, w_mk, w_mv):
    B, M, _ = mem.shape
    mn = rmsnorm(mem, g)
    return ((mn @ w_mk).reshape(B, M, MEM_HEADS, HEAD_DIM), (mn @ w_mv).reshape(B, M, MEM_HEADS, HEAD_DIM))


def memory_attend(x, g, w_mq, w_mo, mk, mv):
    B, T, _ = x.shape
    q = (rmsnorm(x, g) @ w_mq).reshape(B, T, MEM_HEADS, HEAD_DIM)
    s = jnp.einsum('bthd,bmhd->bhtm', q, mk).astype(F32) * HEAD_DIM ** -0.5
    p = jax.nn.softmax(s, axis=-1).astype(mv.dtype)
    o = jnp.einsum('bhtm,bmhd->bthd', p, mv).reshape(B, T, MEM_W)
    return o @ w_mo


def peer_ffn(x, g, w_pq, sub_k1, sub_k2, u, v):
    B, T, D = x.shape
    n = B * T
    xn = rmsnorm(x, g).reshape(n, D)
    n_pad = -(-n // PEER_BLOCK) * PEER_BLOCK
    xb = jnp.pad(xn, ((0, n_pad - n), (0, 0))).reshape(n_pad // PEER_BLOCK, PEER_BLOCK, D)
    half = PEER_QDIM // 2

    def block(xt):
        q = (xt @ w_pq).astype(F32).reshape(PEER_BLOCK, PEER_HEADS, 2, half)
        s1 = jnp.einsum('thd,kd->thk', q[:, :, 0], sub_k1.astype(F32))
        s2 = jnp.einsum('thd,kd->thk', q[:, :, 1], sub_k2.astype(F32))
        v1, i1 = lax.top_k(s1, PEER_TOPK)
        v2, i2 = lax.top_k(s2, PEER_TOPK)
        cand = (v1[..., :, None] + v2[..., None, :]).reshape(PEER_BLOCK, PEER_HEADS, PEER_TOPK * PEER_TOPK)
        cidx = (i1[..., :, None] * PEER_KEYS + i2[..., None, :]).reshape(PEER_BLOCK, PEER_HEADS, PEER_TOPK * PEER_TOPK)
        top, j = lax.top_k(cand, PEER_TOPK)
        e = jnp.take_along_axis(cidx, j, axis=-1)
        gate = jax.nn.softmax(top, axis=-1)
        act = jax.nn.gelu(jnp.einsum('thkd,td->thk', u[e], xt).astype(F32), approximate=False)
        return jnp.einsum('thk,thkd->td', (gate * act).astype(xt.dtype), v[e])

    out = lax.map(block, xb).reshape(n_pad, D)[:n]
    return out.reshape(B, T, D)


def setup_inputs(seed: int = 0) -> dict:
    key = jax.random.key(seed)
    keys = iter(jax.random.split(key, 64))
    n_pages = PAST_LEN // PAGE_SIZE
    n_phys = (5 * DEC_BATCH * n_pages + 3) // 4
    win_buf = min(WINDOW, PAST_LEN)

    def normal(shape, scale=1.0):
        return scale * jax.random.normal(next(keys), shape, F32)

    def gain(n):
        return 1.0 + 0.01 * normal((DEPTH, n))

    page_table = jax.random.permutation(next(keys), n_phys)[:DEC_BATCH * n_pages]
    page_table = page_table.reshape(DEC_BATCH, n_pages).astype(jnp.int32)
    return {
        'x_prompt': normal((BATCH, SEQ, D_MODEL)),
        'x_sample': normal((DEC_BATCH, DEC_SEQ, D_MODEL)),
        'mem_prompt': normal((BATCH, MEM_TOKENS, D_MODEL)),
        'cache_fox_k': normal((DEPTH, n_phys, PAGE_SIZE, FOX_HEADS, HEAD_DIM)),
        'cache_fox_v': normal((DEPTH, n_phys, PAGE_SIZE, FOX_HEADS, HEAD_DIM)),
        'cache_fox_logf': jax.nn.log_sigmoid(FORGET_BIAS + normal((DEPTH, n_phys, PAGE_SIZE, FOX_HEADS))),
        'cache_cmp_k': normal((DEPTH, n_phys, PAGE_SIZE, NSA_KV_HEADS, HEAD_DIM)),
        'cache_cmp_v': normal((DEPTH, n_phys, PAGE_SIZE, NSA_KV_HEADS, HEAD_DIM)),
        'cache_slc_k': normal((DEPTH, n_phys, PAGE_SIZE, NSA_KV_HEADS, HEAD_DIM)),
        'cache_slc_v': normal((DEPTH, n_phys, PAGE_SIZE, NSA_KV_HEADS, HEAD_DIM)),
        'state_win_k': normal((DEPTH, DEC_BATCH, win_buf, NSA_KV_HEADS, HEAD_DIM)),
        'state_win_v': normal((DEPTH, DEC_BATCH, win_buf, NSA_KV_HEADS, HEAD_DIM)),
        'cache_mem_k': normal((DEPTH, DEC_BATCH, MEM_TOKENS, MEM_HEADS, HEAD_DIM)),
        'cache_mem_v': normal((DEPTH, DEC_BATCH, MEM_TOKENS, MEM_HEADS, HEAD_DIM)),
        'page_table': page_table,
        'g_mix': gain(D_MODEL),
        'w_in': normal((DEPTH, D_MODEL, P_IN), D_MODEL ** -0.5),
        'b_fox_f': FORGET_BIAS + 0.5 * normal((DEPTH, FOX_HEADS)),
        'b_nsa_gate': normal((DEPTH, 3 * NSA_HEADS), 0.01),
        'cmp_pos_k': (1.0 + 0.1 * normal((DEPTH, CMP_BLOCK, HEAD_DIM))) * CMP_BLOCK ** -0.5,
        'cmp_phi_k': normal((DEPTH, HEAD_DIM, HEAD_DIM), HEAD_DIM ** -0.5),
        'cmp_pos_v': (1.0 + 0.1 * normal((DEPTH, CMP_BLOCK, HEAD_DIM))) * CMP_BLOCK ** -0.5,
        'cmp_phi_v': normal((DEPTH, HEAD_DIM, HEAD_DIM), HEAD_DIM ** -0.5),
        'g_fox_out': gain(FOX_W),
        'g_nsa_out': gain(NSA_W),
        'w_out': normal((DEPTH, MIX_W, D_MODEL), MIX_W ** -0.5),
        'g_mem_q': gain(D_MODEL),
        'g_mem_kv': gain(D_MODEL),
        'w_mq': normal((DEPTH, D_MODEL, MEM_W), D_MODEL ** -0.5),
        'w_mk': normal((DEPTH, D_MODEL, MEM_W), D_MODEL ** -0.5),
        'w_mv': normal((DEPTH, D_MODEL, MEM_W), D_MODEL ** -0.5),
        'w_mo': normal((DEPTH, MEM_W, D_MODEL), MEM_W ** -0.5),
        'g_peer': gain(D_MODEL),
        'w_pq': normal((DEPTH, D_MODEL, PEER_HEADS * PEER_QDIM), D_MODEL ** -0.5),
        'peer_subkey_1': normal((DEPTH, PEER_KEYS, PEER_QDIM // 2), (PEER_QDIM // 2) ** -0.5),
        'peer_subkey_2': normal((DEPTH, PEER_KEYS, PEER_QDIM // 2), (PEER_QDIM // 2) ** -0.5),
        'peer_u': normal((DEPTH, PEER_EXPERTS, D_MODEL), D_MODEL ** -0.5),
        'peer_v': normal((DEPTH, PEER_EXPERTS, D_MODEL), PEER_HEADS ** -0.5),
        'g_final': 1.0 + 0.01 * normal((D_MODEL,)),
    }


def reference(x_prompt, x_sample, mem_prompt, cache_fox_k, cache_fox_v, cache_fox_logf,
              cache_cmp_k, cache_cmp_v, cache_slc_k, cache_slc_v, state_win_k, state_win_v,
              cache_mem_k, cache_mem_v, page_table,
              g_mix, w_in, b_fox_f, b_nsa_gate, cmp_pos_k, cmp_phi_k, cmp_pos_v, cmp_phi_v,
              g_fox_out, g_nsa_out, w_out, g_mem_q, g_mem_kv, w_mq, w_mk, w_mv, w_mo,
              g_peer, w_pq, peer_subkey_1, peer_subkey_2, peer_u, peer_v, g_final):
    win_buf = min(WINDOW, PAST_LEN)
    names_p = ('fox_k', 'fox_v', 'fox_logf', 'cmp_k', 'cmp_v', 'slc_k', 'slc_v', 'win_k', 'win_v', 'mem_k', 'mem_v')
    names_s = ('fox_k', 'fox_v', 'fox_logf', 'cmp_k', 'cmp_v', 'slc_k', 'slc_v', 'win_k', 'win_v')
    sp = {n: [] for n in names_p}
    ss = {n: [] for n in names_s}
    xp, xs = x_prompt, x_sample
    for l in range(DEPTH):
        fq, fk, fv, lf, nq, kc, vc, ks, vs, kw, vw, gt = project_mixers(xp, g_mix[l], w_in[l], b_fox_f[l], b_nsa_gate[l])
        fox_o = fox_prompt(fq, fk, fv, lf)
        nsa_o = nsa_prompt(nq, gt, kc, vc, ks, vs, kw, vw, cmp_pos_k[l], cmp_phi_k[l], cmp_pos_v[l], cmp_phi_v[l])
        xp = xp + merge_mixers(fox_o, nsa_o, g_fox_out[l], g_nsa_out[l], w_out[l])
        mk, mv = memory_kv(mem_prompt, g_mem_kv[l], w_mk[l], w_mv[l])
        xp = xp + memory_attend(xp, g_mem_q[l], w_mq[l], w_mo[l], mk, mv)
        xp = xp + peer_ffn(xp, g_peer[l], w_pq[l], peer_subkey_1[l], peer_subkey_2[l], peer_u[l], peer_v[l])
        for name, val in zip(names_p, (fk, fv, lf, kc, vc, ks, vs, last_rows(kw, win_buf), last_rows(vw, win_buf), mk, mv)):
            sp[name].append(val)

        fq, fk, fv, lf, nq, kc, vc, ks, vs, kw, vw, gt = project_mixers(xs, g_mix[l], w_in[l], b_fox_f[l], b_nsa_gate[l])
        kw_all = jnp.concatenate([state_win_k[l].astype(kw.dtype), kw], axis=1)
        vw_all = jnp.concatenate([state_win_v[l].astype(vw.dtype), vw], axis=1)
        fox_o, nsa_o = mixers_sample(fq, fk, fv, lf, nq, gt, kc, vc, ks, vs, kw_all, vw_all, page_table,
                                     cache_fox_k[l], cache_fox_v[l], cache_fox_logf[l],
                                     cache_cmp_k[l], cache_cmp_v[l], cache_slc_k[l], cache_slc_v[l],
                                     cmp_pos_k[l], cmp_phi_k[l], cmp_pos_v[l], cmp_phi_v[l])
        xs = xs + merge_mixers(fox_o, nsa_o, g_fox_out[l], g_nsa_out[l], w_out[l])
        xs = xs + memory_attend(xs, g_mem_q[l], w_mq[l], w_mo[l], cache_mem_k[l], cache_mem_v[l])
        xs = xs + peer_ffn(xs, g_peer[l], w_pq[l], peer_subkey_1[l], peer_subkey_2[l], peer_u[l], peer_v[l])
        for name, val in zip(names_s, (fk, fv, lf, kc, vc, ks, vs, kw_all[:, -win_buf:], vw_all[:, -win_buf:])):
            ss[name].append(val)

    y_prompt = rmsnorm(xp, g_final)
    y_sample = rmsnorm(xs, g_final)
    sp = {n: jnp.stack(v) for n, v in sp.items()}
    ss = {n: jnp.stack(v) for n, v in ss.items()}
    return (y_prompt, y_sample,
            sp['fox_k'], sp['fox_v'], sp['fox_logf'], sp['cmp_k'], sp['cmp_v'], sp['slc_k'], sp['slc_v'],
            sp['win_k'], sp['win_v'], sp['mem_k'], sp['mem_v'],
            ss['fox_k'], ss['fox_v'], ss['fox_logf'], ss['cmp_k'], ss['cmp_v'], ss['slc_k'], ss['slc_v'],
            ss['win_k'], ss['win_v'])
```

```python
import functools
import math
import jax
import jax.numpy as jnp
from jax import lax
import numpy as np
from jax.experimental import pallas as pl
from jax.experimental.pallas import tpu as pltpu

D_MODEL = 2048
PAST_LEN = 2048
PAGE_SIZE = 128

HEAD_DIM = 128
FOX_HEADS = 8
NSA_HEADS = 8
NSA_KV_HEADS = 2
NSA_GROUP = NSA_HEADS // NSA_KV_HEADS
CMP_STRIDE = 16
CMP_BLOCK = 2 * CMP_STRIDE
SEL_BLOCK = 64
N_SEL = 16
WINDOW = 512
MEM_TOKENS = 256
MEM_HEADS = 4
PEER_HEADS = 8
PEER_KEYS = 128
PEER_EXPERTS = PEER_KEYS * PEER_KEYS
PEER_TOPK = 16
PEER_QDIM = 256
Q_BLOCK = 128
PEER_BLOCK = 128
FORGET_BIAS = 4.0
FORCE_BONUS = 1.0e4
NEG = -1.0e30
EPS = 1e-6

FOX_W = FOX_HEADS * HEAD_DIM
NSA_W = NSA_HEADS * HEAD_DIM
KV_W = NSA_KV_HEADS * HEAD_DIM
MIX_W = FOX_W + NSA_W
MEM_W = MEM_HEADS * HEAD_DIM
IN_SIZES = (FOX_W, FOX_W, FOX_W, FOX_HEADS, NSA_W, KV_W, KV_W, KV_W, KV_W, KV_W, KV_W, 3 * NSA_HEADS)
P_IN = sum(IN_SIZES)

F32 = jnp.float32


def rmsnorm(x, g):
    xf = x.astype(F32)
    y = xf * lax.rsqrt(jnp.mean(xf * xf, axis=-1, keepdims=True) + EPS)
    return (y * g.astype(F32)).astype(x.dtype)


def _rmsnorm_body(x_ref, g_ref, o_ref):
    xf = x_ref[...]
    y = xf * lax.rsqrt(jnp.mean(xf * xf, axis=-1, keepdims=True) + EPS)
    o_ref[...] = y * g_ref[...]


def rmsnorm_pallas(x, g, rows=256):
    shp = x.shape
    d = shp[-1]
    x2 = x.reshape(-1, d)
    n = x2.shape[0]
    out = pl.pallas_call(
        _rmsnorm_body,
        grid=(n // rows,),
        in_specs=[pl.BlockSpec((rows, d), lambda i: (i, 0)),
                  pl.BlockSpec((1, d), lambda i: (0, 0))],
        out_specs=pl.BlockSpec((rows, d), lambda i: (i, 0)),
        out_shape=jax.ShapeDtypeStruct((n, d), F32),
        name="final_rmsnorm",
    )(x2, g.reshape(1, d))
    return out.reshape(shp)


def masked_softmax(s, mask):
    p = jax.nn.softmax(jnp.where(mask, s, NEG), axis=-1)
    return jnp.where(mask, p, 0.0)


def alibi_slopes(n):
    return 2.0 ** (-8.0 * jnp.arange(1, n + 1, dtype=F32) / n)


def sweep_query_blocks(fn, q_arrays, tq):
    T = tq.shape[0]
    nb = T // Q_BLOCK
    xs = tuple(jnp.swapaxes(a.reshape(a.shape[0], nb, Q_BLOCK, *a.shape[2:]), 0, 1) for a in q_arrays)
    out = lax.map(lambda args: fn(*args), xs + (tq.reshape(nb, Q_BLOCK),))
    out = jnp.swapaxes(out, 0, 1)
    return out.reshape(out.shape[0], T, *out.shape[3:])


def last_rows(a, n):
    T = a.shape[1]
    if T < n:
        a = jnp.pad(a, ((0, 0), (n - T, 0)) + ((0, 0),) * (a.ndim - 2))
    return a[:, -n:]


def project_mixers(x, g_mix, w_in, b_fox_f, b_nsa_gate):
    B, T = x.shape[:2]
    xn = rmsnorm(x, g_mix)
    offs = [int(o) for o in np.cumsum(IN_SIZES)[:-1]]
    fq, fk, fv, ff, nq, kc, vc, ks, vs, kw, vw, ng = jnp.split(xn @ w_in, offs, axis=-1)
    hd = lambda a: a.reshape(B, T, -1, HEAD_DIM)
    logf = jax.nn.log_sigmoid(ff.astype(F32) + b_fox_f.astype(F32))
    gates = jax.nn.sigmoid(ng.astype(F32) + b_nsa_gate.astype(F32)).reshape(B, T, NSA_HEADS, 3)
    return (hd(fq), hd(fk), hd(fv), logf, hd(nq), hd(kc), hd(vc), hd(ks), hd(vs), hd(kw), hd(vw), gates)


def fox_attend(q, cq, tq, k, v, ck, tk):
    s = jnp.einsum('bqhd,bkhd->bhqk', q, k).astype(F32) * HEAD_DIM ** -0.5
    s = s + jnp.swapaxes(cq, 1, 2)[..., :, None] - jnp.swapaxes(ck, 1, 2)[..., None, :]
    p = masked_softmax(s, tk[None, :] <= tq[:, None])
    return jnp.einsum('bhqk,bkhd->bqhd', p.astype(v.dtype), v)


def fox_prompt(fq, fk, fv, logf):
    B, S = fq.shape[:2]
    t = jnp.arange(S, dtype=jnp.int32)
    c = jnp.cumsum(logf, axis=1)
    o = sweep_query_blocks(lambda qb, cb, tb: fox_attend(qb, cb, tb, fk, fv, c, t), (fq, c), t)
    return o.reshape(B, S, FOX_W)


def compress(k, w_pos, w_phi):
    B, L, G, D = k.shape
    n_chunk = L // CMP_STRIDE
    c = k[:, :n_chunk * CMP_STRIDE].reshape(B, n_chunk, CMP_STRIDE, G, D)
    blocks = jnp.concatenate([c[:, :-1], c[:, 1:]], axis=2)
    pooled = jnp.einsum('bnlgd,ld->bngd', blocks, w_pos)
    return pooled @ w_phi


def cmp_positions(n_cmp):
    return jnp.arange(n_cmp, dtype=jnp.int32) * CMP_STRIDE + (CMP_BLOCK - 1)


def to_sel_blocks(k):
    B, L, G, D = k.shape
    n = -(-L // SEL_BLOCK)
    k = jnp.pad(k, ((0, 0), (0, n * SEL_BLOCK - L), (0, 0), (0, 0)))
    return k.reshape(B, n, SEL_BLOCK, G, D).transpose(0, 3, 1, 2, 4)


def nsa_attend(q, gates, tq, k_cmp, v_cmp, t_cmp, ks_blk, vs_blk, kw, vw, tw):
    B, T = q.shape[:2]
    G, HG = NSA_KV_HEADS, NSA_GROUP
    qg = q.reshape(B, T, G, HG, HEAD_DIM)
    slope = alibi_slopes(NSA_HEADS).reshape(G, HG)
    scale = HEAD_DIM ** -0.5

    s = jnp.einsum('btghd,bngd->btghn', qg, k_cmp).astype(F32) * scale
    dist = (tq[:, None] - t_cmp[None, :]).astype(F32)
    s = s - slope[None, None, :, :, None] * dist[None, :, None, None, :]
    p_cmp = masked_softmax(s, (t_cmp[None, :] <= tq[:, None])[None, :, None, None, :])
    o_cmp = jnp.einsum('btghn,bngd->btghd', p_cmp.astype(v_cmp.dtype), v_cmp)

    n_cmp, n_slc = k_cmp.shape[1], ks_blk.shape[2]
    jc = jnp.arange(n_cmp)[:, None] * CMP_STRIDE
    js = jnp.arange(n_slc)[None, :] * SEL_BLOCK
    sel_map = ((jc < js + SEL_BLOCK) & (jc + CMP_BLOCK > js)).astype(F32)
    imp = jnp.einsum('btghn,ns->btgs', p_cmp, sel_map)
    blk = jnp.arange(n_slc)[None, :]
    cur = (tq // SEL_BLOCK)[:, None]
    valid = (blk <= cur)[None, :, None, :]
    forced = ((blk == 0) | (blk == cur) | (blk == cur - 1))[None, :, None, :]
    score = jnp.where(valid, imp + jnp.where(forced, FORCE_BONUS, 0.0), NEG)
    top_s, idx = lax.top_k(score, min(N_SEL, n_slc))
    picked = top_s > 0.5 * NEG

    bi = jnp.arange(B)[:, None, None, None]
    gi = jnp.arange(G)[None, None, :, None]
    k_sel = ks_blk[bi, gi, idx]
    v_sel = vs_blk[bi, gi, idx]
    kpos = idx[..., None] * SEL_BLOCK + jnp.arange(SEL_BLOCK, dtype=jnp.int32)
    s = jnp.einsum('btghd,btgkld->btghkl', qg, k_sel).astype(F32) * scale
    dist = (tq[None, :, None, None, None] - kpos).astype(F32)
    s = s - slope[None, None, :, :, None, None] * dist[:, :, :, None]
    m = (picked[..., None] & (kpos <= tq[None, :, None, None, None]))[:, :, :, None]
    sf = s.reshape(B, T, G, HG, -1)
    p = masked_softmax(sf, jnp.broadcast_to(m, s.shape).reshape(sf.shape)).reshape(s.shape)
    o_sel = jnp.einsum('btghkl,btgkld->btghd', p.astype(v_sel.dtype), v_sel)

    s = jnp.einsum('btghd,bkgd->btghk', qg, kw).astype(F32) * scale
    dwin = tq[:, None] - tw[None, :]
    s = s - slope[None, None, :, :, None] * dwin.astype(F32)[None, :, None, None, :]
    mw = (dwin >= 0) & (dwin < WINDOW) & (tw[None, :] >= 0)
    p = masked_softmax(s, mw[None, :, None, None, :])
    o_win = jnp.einsum('btghk,bkgd->btghd', p.astype(vw.dtype), vw)

    g = gates.reshape(B, T, G, HG, 3)
    o = g[..., 0:1] * o_cmp + g[..., 1:2] * o_sel + g[..., 2:3] * o_win
    return o.reshape(B, T, NSA_W).astype(q.dtype)


def nsa_prompt(nq, gates, kc, vc, ks, vs, kw, vw, cmp_pos_k, cmp_phi_k, cmp_pos_v, cmp_phi_v):
    S = nq.shape[1]
    t_all = jnp.arange(S, dtype=jnp.int32)

    def one_seq(args):
        q1, g1, kc1, vc1, ks1, vs1, kw1, vw1 = [a[None] for a in args]
        k_cmp = compress(kc1, cmp_pos_k, cmp_phi_k)
        v_cmp = compress(vc1, cmp_pos_v, cmp_phi_v)
        t_cmp = cmp_positions(k_cmp.shape[1])
        ks_blk, vs_blk = to_sel_blocks(ks1), to_sel_blocks(vs1)
        pad = ((0, 0), (WINDOW, 0), (0, 0), (0, 0))
        kw_pad, vw_pad = jnp.pad(kw1, pad), jnp.pad(vw1, pad)

        def block(qb, gb, tb):
            start = tb[0]
            kwb = lax.dynamic_slice_in_dim(kw_pad, start, WINDOW + Q_BLOCK, axis=1)
            vwb = lax.dynamic_slice_in_dim(vw_pad, start, WINDOW + Q_BLOCK, axis=1)
            tw = start - WINDOW + jnp.arange(WINDOW + Q_BLOCK, dtype=jnp.int32)
            return nsa_attend(qb, gb, tb, k_cmp, v_cmp, t_cmp, ks_blk, vs_blk, kwb, vwb, tw)

        return sweep_query_blocks(block, (q1, g1), t_all)[0]

    return lax.map(one_seq, (nq, gates, kc, vc, ks, vs, kw, vw))


def mixers_sample(fq, fk, fv, lf, nq, gates, kc, vc, ks, vs, kw_all, vw_all, page_table,
                  fox_k_pool, fox_v_pool, fox_lf_pool, cmp_k_pool, cmp_v_pool, slc_k_pool, slc_v_pool,
                  cmp_pos_k, cmp_phi_k, cmp_pos_v, cmp_phi_v):
    T = fq.shape[1]
    L = PAST_LEN + T
    tq = PAST_LEN + jnp.arange(T, dtype=jnp.int32)
    tk = jnp.arange(L, dtype=jnp.int32)
    n_win = kw_all.shape[1]
    tw = L - n_win + jnp.arange(n_win, dtype=jnp.int32)

    def one_seq(args):
        fq1, fk1, fv1, lf1, nq1, g1, kc1, vc1, ks1, vs1, kw1, vw1, pages = args

        def full(pool, new):
            past = pool[pages].reshape(PAST_LEN, *pool.shape[2:])
            return jnp.concatenate([past.astype(new.dtype), new], axis=0)[None]

        c = jnp.cumsum(full(fox_lf_pool, lf1), axis=1)
        fox_o = fox_attend(fq1[None], c[:, PAST_LEN:], tq, full(fox_k_pool, fk1), full(fox_v_pool, fv1), c, tk)
        k_cmp = compress(full(cmp_k_pool, kc1), cmp_pos_k, cmp_phi_k)
        v_cmp = compress(full(cmp_v_pool, vc1), cmp_pos_v, cmp_phi_v)
        nsa_o = nsa_attend(nq1[None], g1[None], tq, k_cmp, v_cmp, cmp_positions(k_cmp.shape[1]),
                           to_sel_blocks(full(slc_k_pool, ks1)), to_sel_blocks(full(slc_v_pool, vs1)),
                           kw1[None], vw1[None], tw)
        return fox_o[0].reshape(T, FOX_W), nsa_o[0]

    return lax.map(one_seq, (fq, fk, fv, lf, nq, gates, kc, vc, ks, vs, kw_all, vw_all, page_table))


def merge_mixers(fox_o, nsa_o, g_fox_out, g_nsa_out, w_out):
    y = jnp.concatenate([rmsnorm(fox_o, g_fox_out), rmsnorm(nsa_o, g_nsa_out)], axis=-1)
    return y @ w_out


def memory_kv(mem, g, w_mk, w_mv):
    B, M, _ = mem.shape
    mn = rmsnorm(mem, g)
    return ((mn @ w_mk).reshape(B, M, MEM_HEADS, HEAD_DIM), (mn @ w_mv).reshape(B, M, MEM_HEADS, HEAD_DIM))


def memory_attend(x, g, w_mq, w_mo, mk, mv):
    B, T, _ = x.shape
    q = (rmsnorm(x, g) @ w_mq).reshape(B, T, MEM_HEADS, HEAD_DIM)
    s = jnp.einsum('bthd,bmhd->bhtm', q, mk).astype(F32) * HEAD_DIM ** -0.5
    p = jax.nn.softmax(s, axis=-1).astype(mv.dtype)
    o = jnp.einsum('bhtm,bmhd->bthd', p, mv).reshape(B, T, MEM_W)
    return o @ w_mo


def peer_ffn(x, g, w_pq, sub_k1, sub_k2, u, v):
    B, T, D = x.shape
    n = B * T
    xn = rmsnorm(x, g).reshape(n, D)
    n_pad = -(-n // PEER_BLOCK) * PEER_BLOCK
    xb = jnp.pad(xn, ((0, n_pad - n), (0, 0))).reshape(n_pad // PEER_BLOCK, PEER_BLOCK, D)
    half = PEER_QDIM // 2

    def block(xt):
        q = (xt @ w_pq).astype(F32).reshape(PEER_BLOCK, PEER_HEADS, 2, half)
        s1 = jnp.einsum('thd,kd->thk', q[:, :, 0], sub_k1.astype(F32))
        s2 = jnp.einsum('thd,kd->thk', q[:, :, 1], sub_k2.astype(F32))
        v1, i1 = lax.top_k(s1, PEER_TOPK)
        v2, i2 = lax.top_k(s2, PEER_TOPK)
        cand = (v1[..., :, None] + v2[..., None, :]).reshape(PEER_BLOCK, PEER_HEADS, PEER_TOPK * PEER_TOPK)
        cidx = (i1[..., :, None] * PEER_KEYS + i2[..., None, :]).reshape(PEER_BLOCK, PEER_HEADS, PEER_TOPK * PEER_TOPK)
        top, j = lax.top_k(cand, PEER_TOPK)
        e = jnp.take_along_axis(cidx, j, axis=-1)
        gate = jax.nn.softmax(top, axis=-1)
        act = jax.nn.gelu(jnp.einsum('thkd,td->thk', u[e], xt).astype(F32), approximate=False)
        return jnp.einsum('thk,thkd->td', (gate * act).astype(xt.dtype), v[e])

    out = lax.map(block, xb).reshape(n_pad, D)[:n]
    return out.reshape(B, T, D)


def kernel(x_prompt, x_sample, mem_prompt, cache_fox_k, cache_fox_v, cache_fox_logf,
           cache_cmp_k, cache_cmp_v, cache_slc_k, cache_slc_v, state_win_k, state_win_v,
           cache_mem_k, cache_mem_v, page_table,
           g_mix, w_in, b_fox_f, b_nsa_gate, cmp_pos_k, cmp_phi_k, cmp_pos_v, cmp_phi_v,
           g_fox_out, g_nsa_out, w_out, g_mem_q, g_mem_kv, w_mq, w_mk, w_mv, w_mo,
           g_peer, w_pq, peer_subkey_1, peer_subkey_2, peer_u, peer_v, g_final):
    depth = g_mix.shape[0]
    win_buf = min(WINDOW, PAST_LEN)
    names_p = ('fox_k', 'fox_v', 'fox_logf', 'cmp_k', 'cmp_v', 'slc_k', 'slc_v', 'win_k', 'win_v', 'mem_k', 'mem_v')
    names_s = ('fox_k', 'fox_v', 'fox_logf', 'cmp_k', 'cmp_v', 'slc_k', 'slc_v', 'win_k', 'win_v')
    sp = {n: [] for n in names_p}
    ss = {n: [] for n in names_s}
    xp, xs = x_prompt, x_sample
    for l in range(depth):
        fq, fk, fv, lf, nq, kc, vc, ks, vs, kw, vw, gt = project_mixers(xp, g_mix[l], w_in[l], b_fox_f[l], b_nsa_gate[l])
        fox_o = fox_prompt(fq, fk, fv, lf)
        nsa_o = nsa_prompt(nq, gt, kc, vc, ks, vs, kw, vw, cmp_pos_k[l], cmp_phi_k[l], cmp_pos_v[l], cmp_phi_v[l])
        xp = xp + merge_mixers(fox_o, nsa_o, g_fox_out[l], g_nsa_out[l], w_out[l])
        mk, mv = memory_kv(mem_prompt, g_mem_kv[l], w_mk[l], w_mv[l])
        xp = xp + memory_attend(xp, g_mem_q[l], w_mq[l], w_mo[l], mk, mv)
        xp = xp + peer_ffn(xp, g_peer[l], w_pq[l], peer_subkey_1[l], peer_subkey_2[l], peer_u[l], peer_v[l])
        for name, val in zip(names_p, (fk, fv, lf, kc, vc, ks, vs, last_rows(kw, win_buf), last_rows(vw, win_buf), mk, mv)):
            sp[name].append(val)

        fq, fk, fv, lf, nq, kc, vc, ks, vs, kw, vw, gt = project_mixers(xs, g_mix[l], w_in[l], b_fox_f[l], b_nsa_gate[l])
        kw_all = jnp.concatenate([state_win_k[l].astype(kw.dtype), kw], axis=1)
        vw_all = jnp.concatenate([state_win_v[l].astype(vw.dtype), vw], axis=1)
        fox_o, nsa_o = mixers_sample(fq, fk, fv, lf, nq, gt, kc, vc, ks, vs, kw_all, vw_all, page_table,
                                     cache_fox_k[l], cache_fox_v[l], cache_fox_logf[l],
                                     cache_cmp_k[l], cache_cmp_v[l], cache_slc_k[l], cache_slc_v[l],
                                     cmp_pos_k[l], cmp_phi_k[l], cmp_pos_v[l], cmp_phi_v[l])
        xs = xs + merge_mixers(fox_o, nsa_o, g_fox_out[l], g_nsa_out[l], w_out[l])
        xs = xs + memory_attend(xs, g_mem_q[l], w_mq[l], w_mo[l], cache_mem_k[l], cache_mem_v[l])
        xs = xs + peer_ffn(xs, g_peer[l], w_pq[l], peer_subkey_1[l], peer_subkey_2[l], peer_u[l], peer_v[l])
        for name, val in zip(names_s, (fk, fv, lf, kc, vc, ks, vs, kw_all[:, -win_buf:], vw_all[:, -win_buf:])):
            ss[name].append(val)

    y_prompt = rmsnorm_pallas(xp, g_final)
    y_sample = rmsnorm_pallas(xs, g_final)
    sp = {n: jnp.stack(v) for n, v in sp.items()}
    ss = {n: jnp.stack(v) for n, v in ss.items()}
    return (y_prompt, y_sample,
            sp['fox_k'], sp['fox_v'], sp['fox_logf'], sp['cmp_k'], sp['cmp_v'], sp['slc_k'], sp['slc_v'],
            sp['win_k'], sp['win_v'], sp['mem_k'], sp['mem_v'],
            ss['fox_k'], ss['fox_v'], ss['fox_logf'], ss['cmp_k'], ss['cmp_v'], ss['slc_k'], ss['slc_v'],
            ss['win_k'], ss['win_v'])
```

```python
import functools
import math
import jax
import jax.numpy as jnp
from jax import lax
import numpy as np
from jax.experimental import pallas as pl
from jax.experimental.pallas import tpu as pltpu

D_MODEL = 2048
PAST_LEN = 2048
PAGE_SIZE = 128

HEAD_DIM = 128
FOX_HEADS = 8
NSA_HEADS = 8
NSA_KV_HEADS = 2
NSA_GROUP = NSA_HEADS // NSA_KV_HEADS
CMP_STRIDE = 16
CMP_BLOCK = 2 * CMP_STRIDE
SEL_BLOCK = 64
N_SEL = 16
WINDOW = 512
MEM_TOKENS = 256
MEM_HEADS = 4
PEER_HEADS = 8
PEER_KEYS = 128
PEER_EXPERTS = PEER_KEYS * PEER_KEYS
PEER_TOPK = 16
PEER_QDIM = 256
Q_BLOCK = 128
PEER_BLOCK = 128
FORGET_BIAS = 4.0
FORCE_BONUS = 1.0e4
NEG = -1.0e30
EPS = 1e-6

FOX_W = FOX_HEADS * HEAD_DIM
NSA_W = NSA_HEADS * HEAD_DIM
KV_W = NSA_KV_HEADS * HEAD_DIM
MIX_W = FOX_W + NSA_W
MEM_W = MEM_HEADS * HEAD_DIM
IN_SIZES = (FOX_W, FOX_W, FOX_W, FOX_HEADS, NSA_W, KV_W, KV_W, KV_W, KV_W, KV_W, KV_W, 3 * NSA_HEADS)
P_IN = sum(IN_SIZES)

F32 = jnp.float32
BF16 = jnp.bfloat16
NEG_INF = float("-inf")
VMEM_LIMIT = 56 * 1024 * 1024
PEER_TOKEN_TILE = 512
PEER_EXPERT_TILE = 512


def rmsnorm(x, g):
    xf = x.astype(F32)
    y = xf * lax.rsqrt(jnp.mean(xf * xf, axis=-1, keepdims=True) + EPS)
    return (y * g.astype(F32)).astype(x.dtype)


def _rmsnorm_body(x_ref, g_ref, o_ref):
    xf = x_ref[...]
    y = xf * lax.rsqrt(jnp.mean(xf * xf, axis=-1, keepdims=True) + EPS)
    o_ref[...] = y * g_ref[...]


def rmsnorm_pallas(x, g, rows=256):
    shp = x.shape
    d = shp[-1]
    x2 = x.reshape(-1, d)
    n = x2.shape[0]
    out = pl.pallas_call(
        _rmsnorm_body,
        grid=(n // rows,),
        in_specs=[pl.BlockSpec((rows, d), lambda i: (i, 0)),
                  pl.BlockSpec((1, d), lambda i: (0, 0))],
        out_specs=pl.BlockSpec((rows, d), lambda i: (i, 0)),
        out_shape=jax.ShapeDtypeStruct((n, d), F32),
        name="final_rmsnorm",
    )(x2, g.reshape(1, d))
    return out.reshape(shp)


def masked_softmax(s, mask):
    p = jax.nn.softmax(jnp.where(mask, s, NEG), axis=-1)
    return jnp.where(mask, p, 0.0)


def alibi_slopes(n):
    return 2.0 ** (-8.0 * jnp.arange(1, n + 1, dtype=F32) / n)


def sweep_query_blocks(fn, q_arrays, tq):
    T = tq.shape[0]
    nb = T // Q_BLOCK
    xs = tuple(jnp.swapaxes(a.reshape(a.shape[0], nb, Q_BLOCK, *a.shape[2:]), 0, 1) for a in q_arrays)
    out = lax.map(lambda args: fn(*args), xs + (tq.reshape(nb, Q_BLOCK),))
    out = jnp.swapaxes(out, 0, 1)
    return out.reshape(out.shape[0], T, *out.shape[3:])


def last_rows(a, n):
    T = a.shape[1]
    if T < n:
        a = jnp.pad(a, ((0, 0), (n - T, 0)) + ((0, 0),) * (a.ndim - 2))
    return a[:, -n:]


def project_mixers(x, g_mix, w_in, b_fox_f, b_nsa_gate):
    B, T = x.shape[:2]
    xn = rmsnorm(x, g_mix)
    offs = [int(o) for o in np.cumsum(IN_SIZES)[:-1]]
    fq, fk, fv, ff, nq, kc, vc, ks, vs, kw, vw, ng = jnp.split(xn @ w_in, offs, axis=-1)
    hd = lambda a: a.reshape(B, T, -1, HEAD_DIM)
    logf = jax.nn.log_sigmoid(ff.astype(F32) + b_fox_f.astype(F32))
    gates = jax.nn.sigmoid(ng.astype(F32) + b_nsa_gate.astype(F32)).reshape(B, T, NSA_HEADS, 3)
    return (hd(fq), hd(fk), hd(fv), logf, hd(nq), hd(kc), hd(vc), hd(ks), hd(vs), hd(kw), hd(vw), gates)


def fox_attend(q, cq, tq, k, v, ck, tk):
    s = jnp.einsum('bqhd,bkhd->bhqk', q, k).astype(F32) * HEAD_DIM ** -0.5
    s = s + jnp.swapaxes(cq, 1, 2)[..., :, None] - jnp.swapaxes(ck, 1, 2)[..., None, :]
    p = masked_softmax(s, tk[None, :] <= tq[:, None])
    return jnp.einsum('bhqk,bkhd->bqhd', p.astype(v.dtype), v)


def fox_prompt(fq, fk, fv, logf):
    B, S = fq.shape[:2]
    t = jnp.arange(S, dtype=jnp.int32)
    c = jnp.cumsum(logf, axis=1)
    o = sweep_query_blocks(lambda qb, cb, tb: fox_attend(qb, cb, tb, fk, fv, c, t), (fq, c), t)
    return o.reshape(B, S, FOX_W)


def compress(k, w_pos, w_phi):
    B, L, G, D = k.shape
    n_chunk = L // CMP_STRIDE
    c = k[:, :n_chunk * CMP_STRIDE].reshape(B, n_chunk, CMP_STRIDE, G, D)
    blocks = jnp.concatenate([c[:, :-1], c[:, 1:]], axis=2)
    pooled = jnp.einsum('bnlgd,ld->bngd', blocks, w_pos)
    return pooled @ w_phi


def cmp_positions(n_cmp):
    return jnp.arange(n_cmp, dtype=jnp.int32) * CMP_STRIDE + (CMP_BLOCK - 1)


def to_sel_blocks(k):
    B, L, G, D = k.shape
    n = -(-L // SEL_BLOCK)
    k = jnp.pad(k, ((0, 0), (0, n * SEL_BLOCK - L), (0, 0), (0, 0)))
    return k.reshape(B, n, SEL_BLOCK, G, D).transpose(0, 3, 1, 2, 4)


def nsa_attend(q, gates, tq, k_cmp, v_cmp, t_cmp, ks_blk, vs_blk, kw, vw, tw):
    B, T = q.shape[:2]
    G, HG = NSA_KV_HEADS, NSA_GROUP
    qg = q.reshape(B, T, G, HG, HEAD_DIM)
    slope = alibi_slopes(NSA_HEADS).reshape(G, HG)
    scale = HEAD_DIM ** -0.5

    s = jnp.einsum('btghd,bngd->btghn', qg, k_cmp).astype(F32) * scale
    dist = (tq[:, None] - t_cmp[None, :]).astype(F32)
    s = s - slope[None, None, :, :, None] * dist[None, :, None, None, :]
    p_cmp = masked_softmax(s, (t_cmp[None, :] <= tq[:, None])[None, :, None, None, :])
    o_cmp = jnp.einsum('btghn,bngd->btghd', p_cmp.astype(v_cmp.dtype), v_cmp)

    n_cmp, n_slc = k_cmp.shape[1], ks_blk.shape[2]
    jc = jnp.arange(n_cmp)[:, None] * CMP_STRIDE
    js = jnp.arange(n_slc)[None, :] * SEL_BLOCK
    sel_map = ((jc < js + SEL_BLOCK) & (jc + CMP_BLOCK > js)).astype(F32)
    imp = jnp.einsum('btghn,ns->btgs', p_cmp, sel_map)
    blk = jnp.arange(n_slc)[None, :]
    cur = (tq // SEL_BLOCK)[:, None]
    valid = (blk <= cur)[None, :, None, :]
    forced = ((blk == 0) | (blk == cur) | (blk == cur - 1))[None, :, None, :]
    score = jnp.where(valid, imp + jnp.where(forced, FORCE_BONUS, 0.0), NEG)
    top_s, idx = lax.top_k(score, min(N_SEL, n_slc))
    picked = top_s > 0.5 * NEG

    bi = jnp.arange(B)[:, None, None, None]
    gi = jnp.arange(G)[None, None, :, None]
    k_sel = ks_blk[bi, gi, idx]
    v_sel = vs_blk[bi, gi, idx]
    kpos = idx[..., None] * SEL_BLOCK + jnp.arange(SEL_BLOCK, dtype=jnp.int32)
    s = jnp.einsum('btghd,btgkld->btghkl', qg, k_sel).astype(F32) * scale
    dist = (tq[None, :, None, None, None] - kpos).astype(F32)
    s = s - slope[None, None, :, :, None, None] * dist[:, :, :, None]
    m = (picked[..., None] & (kpos <= tq[None, :, None, None, None]))[:, :, :, None]
    sf = s.reshape(B, T, G, HG, -1)
    p = masked_softmax(sf, jnp.broadcast_to(m, s.shape).reshape(sf.shape)).reshape(s.shape)
    o_sel = jnp.einsum('btghkl,btgkld->btghd', p.astype(v_sel.dtype), v_sel)

    s = jnp.einsum('btghd,bkgd->btghk', qg, kw).astype(F32) * scale
    dwin = tq[:, None] - tw[None, :]
    s = s - slope[None, None, :, :, None] * dwin.astype(F32)[None, :, None, None, :]
    mw = (dwin >= 0) & (dwin < WINDOW) & (tw[None, :] >= 0)
    p = masked_softmax(s, mw[None, :, None, None, :])
    o_win = jnp.einsum('btghk,bkgd->btghd', p.astype(vw.dtype), vw)

    g = gates.reshape(B, T, G, HG, 3)
    o = g[..., 0:1] * o_cmp + g[..., 1:2] * o_sel + g[..., 2:3] * o_win
    return o.reshape(B, T, NSA_W).astype(q.dtype)


def nsa_prompt(nq, gates, kc, vc, ks, vs, kw, vw, cmp_pos_k, cmp_phi_k, cmp_pos_v, cmp_phi_v):
    S = nq.shape[1]
    t_all = jnp.arange(S, dtype=jnp.int32)

    def one_seq(args):
        q1, g1, kc1, vc1, ks1, vs1, kw1, vw1 = [a[None] for a in args]
        k_cmp = compress(kc1, cmp_pos_k, cmp_phi_k)
        v_cmp = compress(vc1, cmp_pos_v, cmp_phi_v)
        t_cmp = cmp_positions(k_cmp.shape[1])
        ks_blk, vs_blk = to_sel_blocks(ks1), to_sel_blocks(vs1)
        pad = ((0, 0), (WINDOW, 0), (0, 0), (0, 0))
        kw_pad, vw_pad = jnp.pad(kw1, pad), jnp.pad(vw1, pad)

        def block(qb, gb, tb):
            start = tb[0]
            kwb = lax.dynamic_slice_in_dim(kw_pad, start, WINDOW + Q_BLOCK, axis=1)
            vwb = lax.dynamic_slice_in_dim(vw_pad, start, WINDOW + Q_BLOCK, axis=1)
            tw = start - WINDOW + jnp.arange(WINDOW + Q_BLOCK, dtype=jnp.int32)
            return nsa_attend(qb, gb, tb, k_cmp, v_cmp, t_cmp, ks_blk, vs_blk, kwb, vwb, tw)

        return sweep_query_blocks(block, (q1, g1), t_all)[0]

    return lax.map(one_seq, (nq, gates, kc, vc, ks, vs, kw, vw))


def mixers_sample(fq, fk, fv, lf, nq, gates, kc, vc, ks, vs, kw_all, vw_all, page_table,
                  fox_k_pool, fox_v_pool, fox_lf_pool, cmp_k_pool, cmp_v_pool, slc_k_pool, slc_v_pool,
                  cmp_pos_k, cmp_phi_k, cmp_pos_v, cmp_phi_v):
    T = fq.shape[1]
    L = PAST_LEN + T
    tq = PAST_LEN + jnp.arange(T, dtype=jnp.int32)
    tk = jnp.arange(L, dtype=jnp.int32)
    n_win = kw_all.shape[1]
    tw = L - n_win + jnp.arange(n_win, dtype=jnp.int32)

    def one_seq(args):
        fq1, fk1, fv1, lf1, nq1, g1, kc1, vc1, ks1, vs1, kw1, vw1, pages = args

        def full(pool, new):
            past = pool[pages].reshape(PAST_LEN, *pool.shape[2:])
            return jnp.concatenate([past.astype(new.dtype), new], axis=0)[None]

        c = jnp.cumsum(full(fox_lf_pool, lf1), axis=1)
        fox_o = fox_attend(fq1[None], c[:, PAST_LEN:], tq, full(fox_k_pool, fk1), full(fox_v_pool, fv1), c, tk)
        k_cmp = compress(full(cmp_k_pool, kc1), cmp_pos_k, cmp_phi_k)
        v_cmp = compress(full(cmp_v_pool, vc1), cmp_pos_v, cmp_phi_v)
        nsa_o = nsa_attend(nq1[None], g1[None], tq, k_cmp, v_cmp, cmp_positions(k_cmp.shape[1]),
                           to_sel_blocks(full(slc_k_pool, ks1)), to_sel_blocks(full(slc_v_pool, vs1)),
                           kw1[None], vw1[None], tw)
        return fox_o[0].reshape(T, FOX_W), nsa_o[0]

    return lax.map(one_seq, (fq, fk, fv, lf, nq, gates, kc, vc, ks, vs, kw_all, vw_all, page_table))


def merge_mixers(fox_o, nsa_o, g_fox_out, g_nsa_out, w_out):
    y = jnp.concatenate([rmsnorm(fox_o, g_fox_out), rmsnorm(nsa_o, g_nsa_out)], axis=-1)
    return y @ w_out


def memory_kv(mem, g, w_mk, w_mv):
    B, M, _ = mem.shape
    mn = rmsnorm(mem, g)
    return ((mn @ w_mk).reshape(B, M, MEM_HEADS, HEAD_DIM), (mn @ w_mv).reshape(B, M, MEM_HEADS, HEAD_DIM))


def memory_attend(x, g, w_mq, w_mo, mk, mv):
    B, T, _ = x.shape
    q = (rmsnorm(x, g) @ w_mq).reshape(B, T, MEM_HEADS, HEAD_DIM)
    s = jnp.einsum('bthd,bmhd->bhtm', q, mk).astype(F32) * HEAD_DIM ** -0.5
    p = jax.nn.softmax(s, axis=-1).astype(mv.dtype)
    o = jnp.einsum('bhtm,bmhd->bthd', p, mv).reshape(B, T, MEM_W)
    return o @ w_mo


def _top16_rows(s):
    iota = lax.broadcasted_iota(jnp.int32, s.shape, 0)
    vals = []
    for _ in range(PEER_TOPK):
        m = jnp.max(s, axis=0, keepdims=True)
        idx = jnp.min(jnp.where(s == m, iota, PEER_KEYS), axis=0, keepdims=True)
        s = jnp.where(iota == idx, NEG_INF, s)
        vals.append(m)
    return s == NEG_INF, jnp.concatenate(vals, axis=0)


def _peer_scores_body(x_ref, g_ref, wq_ref, k1_ref, k2_ref,
                      xn_ref, a_ref, b_ref, e1_ref, e2_ref, tau_ref):
    h = pl.program_id(1)

    @pl.when(h == 0)
    def _():
        xf = x_ref[...]
        y = xf * lax.rsqrt(jnp.mean(xf * xf, axis=-1, keepdims=True) + EPS) * g_ref[...]
        xn_ref[...] = y.astype(BF16)

    qT = lax.dot_general(wq_ref[...], xn_ref[...], (((1,), (1,)), ((), ())), preferred_element_type=F32)
    half = qT.shape[0] // 2
    s1 = jnp.dot(k1_ref[...], qT[:half].astype(BF16), preferred_element_type=F32)
    s2 = jnp.dot(k2_ref[...], qT[half:].astype(BF16), preferred_element_type=F32)
    mem1, v1 = _top16_rows(s1)
    mem2, v2 = _top16_rows(s2)
    pieces = [v1[0:1] + v2]
    for r in range(1, 8):
        pieces.append(v1[r:r + 1] + v2[0:8])
    pieces.append(v1[8:16] + v2[0:1])
    c = jnp.concatenate(pieces, axis=0)
    rank = jnp.zeros(c.shape, F32)
    for j in range(c.shape[0]):
        rank = rank + jnp.where(c[j:j + 1] > c, 1.0, 0.0)
    tau = jnp.min(jnp.where(rank <= PEER_TOPK - 1, c, jnp.inf), axis=0, keepdims=True)
    z = jnp.sum(jnp.where(c >= tau, jnp.exp(c - c[0:1]), 0.0), axis=0, keepdims=True)
    a_ref[0] = jnp.where(mem1, s1, NEG_INF)
    b_ref[0] = jnp.where(mem2, s2, NEG_INF)
    e1_ref[0] = jnp.exp(s1 - v1[0:1]) / z
    e2_ref[0] = jnp.exp(s2 - v2[0:1])
    tau_ref[0] = tau


def peer_scores(x2, g, wqT_bf, k1_bf, k2_bf, tt):
    n, d = x2.shape
    H = PEER_HEADS
    qd = wqT_bf.shape[0] // H
    return pl.pallas_call(
        _peer_scores_body,
        grid=(n // tt, H),
        in_specs=[pl.BlockSpec((tt, d), lambda i, h: (i, 0)),
                  pl.BlockSpec((1, d), lambda i, h: (0, 0)),
                  pl.BlockSpec((qd, d), lambda i, h: (h, 0)),
                  pl.BlockSpec((PEER_KEYS, qd // 2), lambda i, h: (0, 0)),
                  pl.BlockSpec((PEER_KEYS, qd // 2), lambda i, h: (0, 0))],
        out_specs=[pl.BlockSpec((tt, d), lambda i, h: (i, 0))]
                  + [pl.BlockSpec((1, PEER_KEYS, tt), lambda i, h: (h, 0, i))] * 4
                  + [pl.BlockSpec((1, 1, tt), lambda i, h: (h, 0, i))],
        out_shape=[jax.ShapeDtypeStruct((n, d), BF16)]
                  + [jax.ShapeDtypeStruct((H, PEER_KEYS, n), F32)] * 4
                  + [jax.ShapeDtypeStruct((H, 1, n), F32)],
        compiler_params=pltpu.CompilerParams(dimension_semantics=("arbitrary", "arbitrary"),
                                             vmem_limit_bytes=VMEM_LIMIT),
        name="peer_scores",
    )(x2, g.reshape(1, d), wqT_bf, k1_bf, k2_bf)


def _gelu_exact(x):
    return 0.5 * x * (1.0 + lax.erf(x * (2.0 ** -0.5)))


def _peer_dense_body(xn_ref, u_ref, vt_ref, a_ref, b_ref, e1_ref, e2_ref, tau_ref, o_ref, acc_ref, *, te):
    j = pl.program_id(1)

    @pl.when(j == 0)
    def _():
        acc_ref[...] = jnp.zeros(acc_ref.shape, F32)

    hT = lax.dot_general(u_ref[...], xn_ref[...], (((1,), (1,)), ((), ())), preferred_element_type=F32)
    na = te // PEER_KEYS
    rows = []
    for al in range(na):
        a = j * na + al
        w = jnp.zeros((PEER_KEYS, hT.shape[1]), F32)
        for h in range(PEER_HEADS):
            val = a_ref[h, pl.ds(a, 1), :] + b_ref[h]
            gate = e1_ref[h, pl.ds(a, 1), :] * e2_ref[h]
            w = w + jnp.where(val >= tau_ref[h], gate, 0.0)
        act = _gelu_exact(hT[al * PEER_KEYS:(al + 1) * PEER_KEYS])
        rows.append((w * act).astype(BF16))
    pT = jnp.concatenate(rows, axis=0)
    acc_ref[...] += jnp.dot(vt_ref[...], pT, preferred_element_type=F32)

    @pl.when(j == pl.num_programs(1) - 1)
    def _():
        o_ref[...] = acc_ref[...].T


def peer_dense(xn_bf, u_bf, vT_bf, a, b, e1, e2, tau, tt, te):
    n, d = xn_bf.shape
    ne = u_bf.shape[0]
    H = PEER_HEADS
    return pl.pallas_call(
        functools.partial(_peer_dense_body, te=te),
        grid=(n // tt, ne // te),
        in_specs=[pl.BlockSpec((tt, d), lambda i, j: (i, 0)),
                  pl.BlockSpec((te, d), lambda i, j: (j, 0)),
                  pl.BlockSpec((d, te), lambda i, j: (0, j))]
                 + [pl.BlockSpec((H, PEER_KEYS, tt), lambda i, j: (0, 0, i))] * 4
                 + [pl.BlockSpec((H, 1, tt), lambda i, j: (0, 0, i))],
        out_specs=pl.BlockSpec((tt, d), lambda i, j: (i, 0)),
        out_shape=jax.ShapeDtypeStruct((n, d), F32),
        scratch_shapes=[pltpu.VMEM((d, tt), F32)],
        compiler_params=pltpu.CompilerParams(dimension_semantics=("arbitrary", "arbitrary"),
                                             vmem_limit_bytes=VMEM_LIMIT),
        name="peer_dense",
    )(xn_bf, u_bf, vT_bf, a, b, e1, e2, tau)


def peer_ffn(x, g, wqT_bf, k1_bf, k2_bf, u_bf, vT_bf):
    B, T, D = x.shape
    x2 = x.reshape(B * T, D)
    xn_bf, a, b, e1, e2, tau = peer_scores(x2, g, wqT_bf, k1_bf, k2_bf, PEER_TOKEN_TILE)
    out = peer_dense(xn_bf, u_bf, vT_bf, a, b, e1, e2, tau, PEER_TOKEN_TILE, PEER_EXPERT_TILE)
    return out.reshape(B, T, D)


def kernel(x_prompt, x_sample, mem_prompt, cache_fox_k, cache_fox_v, cache_fox_logf,
           cache_cmp_k, cache_cmp_v, cache_slc_k, cache_slc_v, state_win_k, state_win_v,
           cache_mem_k, cache_mem_v, page_table,
           g_mix, w_in, b_fox_f, b_nsa_gate, cmp_pos_k, cmp_phi_k, cmp_pos_v, cmp_phi_v,
           g_fox_out, g_nsa_out, w_out, g_mem_q, g_mem_kv, w_mq, w_mk, w_mv, w_mo,
           g_peer, w_pq, peer_subkey_1, peer_subkey_2, peer_u, peer_v, g_final):
    depth = g_mix.shape[0]
    win_buf = min(WINDOW, PAST_LEN)
    names_p = ('fox_k', 'fox_v', 'fox_logf', 'cmp_k', 'cmp_v', 'slc_k', 'slc_v', 'win_k', 'win_v', 'mem_k', 'mem_v')
    names_s = ('fox_k', 'fox_v', 'fox_logf', 'cmp_k', 'cmp_v', 'slc_k', 'slc_v', 'win_k', 'win_v')
    sp = {n: [] for n in names_p}
    ss = {n: [] for n in names_s}
    xp, xs = x_prompt, x_sample
    for l in range(depth):
        fq, fk, fv, lf, nq, kc, vc, ks, vs, kw, vw, gt = project_mixers(xp, g_mix[l], w_in[l], b_fox_f[l], b_nsa_gate[l])
        fox_o = fox_prompt(fq, fk, fv, lf)
        nsa_o = nsa_prompt(nq, gt, kc, vc, ks, vs, kw, vw, cmp_pos_k[l], cmp_phi_k[l], cmp_pos_v[l], cmp_phi_v[l])
        xp = xp + merge_mixers(fox_o, nsa_o, g_fox_out[l], g_nsa_out[l], w_out[l])
        mk, mv = memory_kv(mem_prompt, g_mem_kv[l], w_mk[l], w_mv[l])
        xp = xp + memory_attend(xp, g_mem_q[l], w_mq[l], w_mo[l], mk, mv)
        peer_w = (w_pq[l].T.astype(BF16), peer_subkey_1[l].astype(BF16), peer_subkey_2[l].astype(BF16),
                  peer_u[l].astype(BF16), peer_v[l].T.astype(BF16))
        xp = xp + peer_ffn(xp, g_peer[l], *peer_w)
        for name, val in zip(names_p, (fk, fv, lf, kc, vc, ks, vs, last_rows(kw, win_buf), last_rows(vw, win_buf), mk, mv)):
            sp[name].append(val)

        fq, fk, fv, lf, nq, kc, vc, ks, vs, kw, vw, gt = project_mixers(xs, g_mix[l], w_in[l], b_fox_f[l], b_nsa_gate[l])
        kw_all = jnp.concatenate([state_win_k[l].astype(kw.dtype), kw], axis=1)
        vw_all = jnp.concatenate([state_win_v[l].astype(vw.dtype), vw], axis=1)
        fox_o, nsa_o = mixers_sample(fq, fk, fv, lf, nq, gt, kc, vc, ks, vs, kw_all, vw_all, page_table,
                                     cache_fox_k[l], cache_fox_v[l], cache_fox_logf[l],
                                     cache_cmp_k[l], cache_cmp_v[l], cache_slc_k[l], cache_slc_v[l],
                                     cmp_pos_k[l], cmp_phi_k[l], cmp_pos_v[l], cmp_phi_v[l])
        xs = xs + merge_mixers(fox_o, nsa_o, g_fox_out[l], g_nsa_out[l], w_out[l])
        xs = xs + memory_attend(xs, g_mem_q[l], w_mq[l], w_mo[l], cache_mem_k[l], cache_mem_v[l])
        xs = xs + peer_ffn(xs, g_peer[l], *peer_w)
        for name, val in zip(names_s, (fk, fv, lf, kc, vc, ks, vs, kw_all[:, -win_buf:], vw_all[:, -win_buf:])):
            ss[name].append(val)

    y_prompt = rmsnorm_pallas(xp, g_final)
    y_sample = rmsnorm_pallas(xs, g_final)
    sp = {n: jnp.stack(v) for n, v in sp.items()}
    ss = {n: jnp.stack(v) for n, v in ss.items()}
    return (y_prompt, y_sample,
            sp['fox_k'], sp['fox_v'], sp['fox_logf'], sp['cmp_k'], sp['cmp_v'], sp['slc_k'], sp['slc_v'],
            sp['win_k'], sp['win_v'], sp['mem_k'], sp['mem_v'],
            ss['fox_k'], ss['fox_v'], ss['fox_logf'], ss['cmp_k'], ss['cmp_v'], ss['slc_k'], ss['slc_v'],
            ss['win_k'], ss['win_v'])
```

```python
import functools
import math
import jax
import jax.numpy as jnp
from jax import lax
import numpy as np
from jax.experimental import pallas as pl
from jax.experimental.pallas import tpu as pltpu

D_MODEL = 2048
PAST_LEN = 2048
PAGE_SIZE = 128

HEAD_DIM = 128
FOX_HEADS = 8
NSA_HEADS = 8
NSA_KV_HEADS = 2
NSA_GROUP = NSA_HEADS // NSA_KV_HEADS
CMP_STRIDE = 16
CMP_BLOCK = 2 * CMP_STRIDE
SEL_BLOCK = 64
N_SEL = 16
WINDOW = 512
MEM_TOKENS = 256
MEM_HEADS = 4
PEER_HEADS = 8
PEER_KEYS = 128
PEER_EXPERTS = PEER_KEYS * PEER_KEYS
PEER_TOPK = 16
PEER_QDIM = 256
Q_BLOCK = 128
PEER_BLOCK = 128
FORGET_BIAS = 4.0
FORCE_BONUS = 1.0e4
NEG = -1.0e30
EPS = 1e-6

FOX_W = FOX_HEADS * HEAD_DIM
NSA_W = NSA_HEADS * HEAD_DIM
KV_W = NSA_KV_HEADS * HEAD_DIM
MIX_W = FOX_W + NSA_W
MEM_W = MEM_HEADS * HEAD_DIM
IN_SIZES = (FOX_W, FOX_W, FOX_W, FOX_HEADS, NSA_W, KV_W, KV_W, KV_W, KV_W, KV_W, KV_W, 3 * NSA_HEADS)
P_IN = sum(IN_SIZES)

F32 = jnp.float32
BF16 = jnp.bfloat16
NEG_INF = float("-inf")
VMEM_LIMIT = 56 * 1024 * 1024
PEER_TOKEN_TILE = 512
PEER_EXPERT_TILE = 512
SLC_CHUNK = 512
NT = (((1,), (1,)), ((), ()))


def rmsnorm(x, g):
    xf = x.astype(F32)
    y = xf * lax.rsqrt(jnp.mean(xf * xf, axis=-1, keepdims=True) + EPS)
    return (y * g.astype(F32)).astype(x.dtype)


def _rmsnorm_body(x_ref, g_ref, o_ref):
    xf = x_ref[...]
    y = xf * lax.rsqrt(jnp.mean(xf * xf, axis=-1, keepdims=True) + EPS)
    o_ref[...] = y * g_ref[...]


def rmsnorm_pallas(x, g, rows=256):
    shp = x.shape
    d = shp[-1]
    x2 = x.reshape(-1, d)
    n = x2.shape[0]
    out = pl.pallas_call(
        _rmsnorm_body,
        grid=(n // rows,),
        in_specs=[pl.BlockSpec((rows, d), lambda i: (i, 0)),
                  pl.BlockSpec((1, d), lambda i: (0, 0))],
        out_specs=pl.BlockSpec((rows, d), lambda i: (i, 0)),
        out_shape=jax.ShapeDtypeStruct((n, d), F32),
        name="final_rmsnorm",
    )(x2, g.reshape(1, d))
    return out.reshape(shp)


def masked_softmax(s, mask):
    p = jax.nn.softmax(jnp.where(mask, s, NEG), axis=-1)
    return jnp.where(mask, p, 0.0)


def alibi_slopes(n):
    return 2.0 ** (-8.0 * jnp.arange(1, n + 1, dtype=F32) / n)


def sweep_query_blocks(fn, q_arrays, tq):
    T = tq.shape[0]
    nb = T // Q_BLOCK
    xs = tuple(jnp.swapaxes(a.reshape(a.shape[0], nb, Q_BLOCK, *a.shape[2:]), 0, 1) for a in q_arrays)
    out = lax.map(lambda args: fn(*args), xs + (tq.reshape(nb, Q_BLOCK),))
    out = jnp.swapaxes(out, 0, 1)
    return out.reshape(out.shape[0], T, *out.shape[3:])


def last_rows(a, n):
    T = a.shape[1]
    if T < n:
        a = jnp.pad(a, ((0, 0), (n - T, 0)) + ((0, 0),) * (a.ndim - 2))
    return a[:, -n:]


def project_mixers(x, g_mix, w_in, b_fox_f, b_nsa_gate):
    B, T = x.shape[:2]
    xn = rmsnorm(x, g_mix)
    offs = [int(o) for o in np.cumsum(IN_SIZES)[:-1]]
    fq, fk, fv, ff, nq, kc, vc, ks, vs, kw, vw, ng = jnp.split(xn @ w_in, offs, axis=-1)
    hd = lambda a: a.reshape(B, T, -1, HEAD_DIM)
    logf = jax.nn.log_sigmoid(ff.astype(F32) + b_fox_f.astype(F32))
    gates = jax.nn.sigmoid(ng.astype(F32) + b_nsa_gate.astype(F32)).reshape(B, T, NSA_HEADS, 3)
    return (hd(fq), hd(fk), hd(fv), logf, hd(nq), hd(kc), hd(vc), hd(ks), hd(vs), hd(kw), hd(vw), gates)


def fox_attend(q, cq, tq, k, v, ck, tk):
    s = jnp.einsum('bqhd,bkhd->bhqk', q, k).astype(F32) * HEAD_DIM ** -0.5
    s = s + jnp.swapaxes(cq, 1, 2)[..., :, None] - jnp.swapaxes(ck, 1, 2)[..., None, :]
    p = masked_softmax(s, tk[None, :] <= tq[:, None])
    return jnp.einsum('bhqk,bkhd->bqhd', p.astype(v.dtype), v)


def fox_prompt(fq, fk, fv, logf):
    B, S = fq.shape[:2]
    t = jnp.arange(S, dtype=jnp.int32)
    c = jnp.cumsum(logf, axis=1)
    o = sweep_query_blocks(lambda qb, cb, tb: fox_attend(qb, cb, tb, fk, fv, c, t), (fq, c), t)
    return o.reshape(B, S, FOX_W)


def compress(k, w_pos, w_phi):
    B, L, G, D = k.shape
    n_chunk = L // CMP_STRIDE
    c = k[:, :n_chunk * CMP_STRIDE].reshape(B, n_chunk, CMP_STRIDE, G, D)
    blocks = jnp.concatenate([c[:, :-1], c[:, 1:]], axis=2)
    pooled = jnp.einsum('bnlgd,ld->bngd', blocks, w_pos)
    return pooled @ w_phi


def cmp_positions(n_cmp):
    return jnp.arange(n_cmp, dtype=jnp.int32) * CMP_STRIDE + (CMP_BLOCK - 1)


def to_sel_blocks(k):
    B, L, G, D = k.shape
    n = -(-L // SEL_BLOCK)
    k = jnp.pad(k, ((0, 0), (0, n * SEL_BLOCK - L), (0, 0), (0, 0)))
    return k.reshape(B, n, SEL_BLOCK, G, D).transpose(0, 3, 1, 2, 4)


def nsa_attend(q, gates, tq, k_cmp, v_cmp, t_cmp, ks_blk, vs_blk, kw, vw, tw):
    B, T = q.shape[:2]
    G, HG = NSA_KV_HEADS, NSA_GROUP
    qg = q.reshape(B, T, G, HG, HEAD_DIM)
    slope = alibi_slopes(NSA_HEADS).reshape(G, HG)
    scale = HEAD_DIM ** -0.5

    s = jnp.einsum('btghd,bngd->btghn', qg, k_cmp).astype(F32) * scale
    dist = (tq[:, None] - t_cmp[None, :]).astype(F32)
    s = s - slope[None, None, :, :, None] * dist[None, :, None, None, :]
    p_cmp = masked_softmax(s, (t_cmp[None, :] <= tq[:, None])[None, :, None, None, :])
    o_cmp = jnp.einsum('btghn,bngd->btghd', p_cmp.astype(v_cmp.dtype), v_cmp)

    n_cmp, n_slc = k_cmp.shape[1], ks_blk.shape[2]
    jc = jnp.arange(n_cmp)[:, None] * CMP_STRIDE
    js = jnp.arange(n_slc)[None, :] * SEL_BLOCK
    sel_map = ((jc < js + SEL_BLOCK) & (jc + CMP_BLOCK > js)).astype(F32)
    imp = jnp.einsum('btghn,ns->btgs', p_cmp, sel_map)
    blk = jnp.arange(n_slc)[None, :]
    cur = (tq // SEL_BLOCK)[:, None]
    valid = (blk <= cur)[None, :, None, :]
    forced = ((blk == 0) | (blk == cur) | (blk == cur - 1))[None, :, None, :]
    score = jnp.where(valid, imp + jnp.where(forced, FORCE_BONUS, 0.0), NEG)
    top_s, idx = lax.top_k(score, min(N_SEL, n_slc))
    picked = top_s > 0.5 * NEG

    bi = jnp.arange(B)[:, None, None, None]
    gi = jnp.arange(G)[None, None, :, None]
    k_sel = ks_blk[bi, gi, idx]
    v_sel = vs_blk[bi, gi, idx]
    kpos = idx[..., None] * SEL_BLOCK + jnp.arange(SEL_BLOCK, dtype=jnp.int32)
    s = jnp.einsum('btghd,btgkld->btghkl', qg, k_sel).astype(F32) * scale
    dist = (tq[None, :, None, None, None] - kpos).astype(F32)
    s = s - slope[None, None, :, :, None, None] * dist[:, :, :, None]
    m = (picked[..., None] & (kpos <= tq[None, :, None, None, None]))[:, :, :, None]
    sf = s.reshape(B, T, G, HG, -1)
    p = masked_softmax(sf, jnp.broadcast_to(m, s.shape).reshape(sf.shape)).reshape(s.shape)
    o_sel = jnp.einsum('btghkl,btgkld->btghd', p.astype(v_sel.dtype), v_sel)

    s = jnp.einsum('btghd,bkgd->btghk', qg, kw).astype(F32) * scale
    dwin = tq[:, None] - tw[None, :]
    s = s - slope[None, None, :, :, None] * dwin.astype(F32)[None, :, None, None, :]
    mw = (dwin >= 0) & (dwin < WINDOW) & (tw[None, :] >= 0)
    p = masked_softmax(s, mw[None, :, None, None, :])
    o_win = jnp.einsum('btghk,bkgd->btghd', p.astype(vw.dtype), vw)

    g = gates.reshape(B, T, G, HG, 3)
    o = g[..., 0:1] * o_cmp + g[..., 1:2] * o_sel + g[..., 2:3] * o_win
    return o.reshape(B, T, NSA_W).astype(q.dtype)


def _masked_softmax_rows(s, mask):
    s = jnp.where(mask, s, NEG)
    e = jnp.exp(s - jnp.max(s, axis=-1, keepdims=True))
    p = e / jnp.sum(e, axis=-1, keepdims=True)
    return jnp.where(mask, p, 0.0)


def _split3(x):
    hi = x.astype(BF16)
    r = x - hi.astype(F32)
    mid = r.astype(BF16)
    lo = (r - mid.astype(F32)).astype(BF16)
    return hi, mid, lo


def _select_blocks(p_sum, t_lane, n_slc_pad, k_sel):
    T = p_sum.shape[0]
    s_iota = lax.broadcasted_iota(jnp.int32, (n_slc_pad, 1), 0)
    n_iota = lax.broadcasted_iota(jnp.int32, (1, p_sum.shape[1]), 1)
    ratio = SEL_BLOCK // CMP_STRIDE
    selT = ((n_iota < ratio * s_iota + ratio) & (n_iota > ratio * s_iota - CMP_BLOCK // CMP_STRIDE)).astype(BF16)
    impT = sum(lax.dot_general(selT, x, NT, preferred_element_type=F32) for x in _split3(p_sum))
    curT = t_lane // SEL_BLOCK
    validT = s_iota <= curT
    forcedT = (s_iota == 0) | (s_iota == curT) | (s_iota == curT - 1)
    scoreT = jnp.where(validT, impT + jnp.where(forcedT, FORCE_BONUS, 0.0), NEG)
    rank = jnp.zeros(scoreT.shape, F32)
    for sp in range(n_slc_pad):
        row = scoreT[sp:sp + 1]
        beats = (row > scoreT) | ((row == scoreT) & (s_iota > sp))
        rank = rank + jnp.where(beats, 1.0, 0.0)
    pickedT = jnp.where(validT & (rank < k_sel), 1.0, 0.0)
    pad = jnp.zeros((128 - n_slc_pad, T), F32)
    return jnp.concatenate([pickedT, pad], axis=0).T


def _nsa_prompt_body(q_ref, gt_ref, kcmp_ref, vcmp_ref, ks_ref, vs_ref, kw_ref, vw_ref, o_ref, *, n_slc):
    i = pl.program_id(1)
    scale = HEAD_DIM ** -0.5
    t_row = i * Q_BLOCK + lax.broadcasted_iota(jnp.int32, (Q_BLOCK, 1), 0)
    t_lane = i * Q_BLOCK + lax.broadcasted_iota(jnp.int32, (1, Q_BLOCK), 1)
    gts = gt_ref[0]
    n_iota = lax.broadcasted_iota(jnp.int32, (1, 128), 1)
    t_cmp = n_iota * CMP_STRIDE + (CMP_BLOCK - 1)
    dist_c = (t_row - t_cmp).astype(F32)
    mask_c = t_cmp <= t_row
    s128 = lax.broadcasted_iota(jnp.int32, (128, 1), 0)
    n_chunks = (i * Q_BLOCK + Q_BLOCK + SLC_CHUNK - 1) // SLC_CHUNK
    w0 = pl.multiple_of(jnp.maximum(i - WINDOW // Q_BLOCK, 0) * Q_BLOCK, Q_BLOCK)
    tw = w0 + lax.broadcasted_iota(jnp.int32, (1, WINDOW + Q_BLOCK), 1)
    dwin = t_row - tw
    mask_w = (dwin >= 0) & (dwin < WINDOW)
    dwin_f = dwin.astype(F32)

    for g in range(NSA_KV_HEADS):
        cols = slice(g * HEAD_DIM, (g + 1) * HEAD_DIM)
        heads = [g * NSA_GROUP + h for h in range(NSA_GROUP)]
        slopes = [2.0 ** -(hd + 1) for hd in heads]
        qs = [q_ref[0, :, hd * HEAD_DIM:(hd + 1) * HEAD_DIM].astype(BF16) for hd in heads]

        kc = kcmp_ref[0, :, cols]
        vc = vcmp_ref[0, :, cols]
        o_cmp = []
        p_sum = jnp.zeros((Q_BLOCK, 128), F32)
        for h in range(NSA_GROUP):
            s = lax.dot_general(qs[h], kc, NT, preferred_element_type=F32) * scale - slopes[h] * dist_c
            p = _masked_softmax_rows(s, mask_c)
            o_cmp.append(jnp.dot(p.astype(BF16), vc, preferred_element_type=F32))
            p_sum = p_sum + p

        picked_bf = _select_blocks(p_sum, t_lane, n_slc, min(N_SEL, n_slc)).astype(BF16)

        def chunk(c, carry):
            ms, ls, accs = carry
            k0 = pl.multiple_of(c * SLC_CHUNK, SLC_CHUNK)
            kb = ks_ref[0, pl.ds(k0, SLC_CHUNK), cols]
            vb = vs_ref[0, pl.ds(k0, SLC_CHUNK), cols]
            kpos = k0 + lax.broadcasted_iota(jnp.int32, (1, SLC_CHUNK), 1)
            expand = ((kpos // SEL_BLOCK) == s128).astype(BF16)
            mask = (jnp.dot(picked_bf, expand, preferred_element_type=F32) > 0.5) & (kpos <= t_row)
            dist = (t_row - kpos).astype(F32)
            new_m, new_l, new_acc = [], [], []
            for h in range(NSA_GROUP):
                s = lax.dot_general(qs[h], kb, NT, preferred_element_type=F32) * scale - slopes[h] * dist
                s = jnp.where(mask, s, NEG)
                m_new = jnp.maximum(ms[h], jnp.max(s, axis=-1, keepdims=True))
                alpha = jnp.exp(ms[h] - m_new)
                p = jnp.where(mask, jnp.exp(s - m_new), 0.0)
                new_l.append(alpha * ls[h] + jnp.sum(p, axis=-1, keepdims=True))
                new_acc.append(alpha * accs[h] + jnp.dot(p.astype(BF16), vb, preferred_element_type=F32))
                new_m.append(m_new)
            return tuple(new_m), tuple(new_l), tuple(new_acc)

        init = (tuple(jnp.full((Q_BLOCK, 1), NEG, F32) for _ in range(NSA_GROUP)),
                tuple(jnp.zeros((Q_BLOCK, 1), F32) for _ in range(NSA_GROUP)),
                tuple(jnp.zeros((Q_BLOCK, HEAD_DIM), F32) for _ in range(NSA_GROUP)))
        _, ls, accs = lax.fori_loop(0, n_chunks, chunk, init)

        kwb = kw_ref[0, pl.ds(w0, WINDOW + Q_BLOCK), cols]
        vwb = vw_ref[0, pl.ds(w0, WINDOW + Q_BLOCK), cols]
        for h in range(NSA_GROUP):
            s = lax.dot_general(qs[h], kwb, NT, preferred_element_type=F32) * scale - slopes[h] * dwin_f
            p = _masked_softmax_rows(s, mask_w)
            o_win = jnp.dot(p.astype(BF16), vwb, preferred_element_type=F32)
            hd = heads[h]
            o = (gts[:, 3 * hd:3 * hd + 1] * o_cmp[h] + gts[:, 3 * hd + 1:3 * hd + 2] * (accs[h] / ls[h])
                 + gts[:, 3 * hd + 2:3 * hd + 3] * o_win)
            o_ref[0, :, hd * HEAD_DIM:(hd + 1) * HEAD_DIM] = o


def nsa_prompt_attend(nq, gates, k_cmp, v_cmp, ks, vs, kw, vw):
    B, S, W = nq.shape
    assert S % SLC_CHUNK == 0 and S >= WINDOW + Q_BLOCK and S // SEL_BLOCK <= 128
    n_slc = S // SEL_BLOCK
    full = lambda: pl.BlockSpec((1, S, KV_W), lambda b, i: (b, 0, 0))
    return pl.pallas_call(
        functools.partial(_nsa_prompt_body, n_slc=n_slc),
        grid=(B, S // Q_BLOCK),
        in_specs=[pl.BlockSpec((1, Q_BLOCK, W), lambda b, i: (b, i, 0)),
                  pl.BlockSpec((1, Q_BLOCK, gates.shape[-1]), lambda b, i: (b, i, 0)),
                  pl.BlockSpec((1, 128, KV_W), lambda b, i: (b, 0, 0)),
                  pl.BlockSpec((1, 128, KV_W), lambda b, i: (b, 0, 0)),
                  full(), full(), full(), full()],
        out_specs=pl.BlockSpec((1, Q_BLOCK, W), lambda b, i: (b, i, 0)),
        out_shape=jax.ShapeDtypeStruct((B, S, W), F32),
        compiler_params=pltpu.CompilerParams(dimension_semantics=("arbitrary", "arbitrary"),
                                             vmem_limit_bytes=VMEM_LIMIT),
        name="nsa_prompt",
    )(nq, gates, k_cmp, v_cmp, ks, vs, kw, vw)


def _pad_cmp(c):
    B, n = c.shape[:2]
    return jnp.pad(c.reshape(B, n, KV_W), ((0, 0), (0, 128 - n), (0, 0))).astype(BF16)


def nsa_prompt(nq, gates, kc, vc, ks, vs, kw, vw, cmp_pos_k, cmp_phi_k, cmp_pos_v, cmp_phi_v):
    B, S = nq.shape[:2]
    flat = lambda a: a.reshape(B, S, KV_W).astype(BF16)
    k_cmp = _pad_cmp(compress(kc, cmp_pos_k, cmp_phi_k))
    v_cmp = _pad_cmp(compress(vc, cmp_pos_v, cmp_phi_v))
    return nsa_prompt_attend(nq.reshape(B, S, NSA_W), gates.reshape(B, S, 3 * NSA_HEADS), k_cmp, v_cmp,
                             flat(ks), flat(vs), flat(kw), flat(vw))


def _bf_round(x):
    return x.astype(BF16).astype(F32)


def _fox_sample_body(pt_ref, q_ref, kn_ref, vn_ref, ck_ref, cn_ref, *refs, n_pages):
    k_refs, v_refs, o_ref = refs[:n_pages], refs[n_pages:2 * n_pages], refs[2 * n_pages]
    T = q_ref.shape[1]
    H = FOX_HEADS
    R = T * H
    scale = HEAD_DIM ** -0.5
    r_iota = lax.broadcasted_iota(jnp.int32, (R, 1), 0)
    c_iota = lax.broadcasted_iota(jnp.int32, (1, FOX_W), 1)
    headmask = (c_iota // HEAD_DIM) == (r_iota % H)
    q = q_ref[0]
    qrep = jnp.concatenate([jnp.broadcast_to(q[t:t + 1], (H, FOX_W)) for t in range(T)], axis=0)
    qbd = jnp.where(headmask, qrep, 0.0).astype(BF16)
    ck = ck_ref[0]
    cn = cn_ref[0]
    cq = jnp.concatenate([cn[:, t:t + 1] for t in range(T)], axis=0)
    s_past = jnp.concatenate(
        [lax.dot_general(qbd, k_refs[p][0].astype(BF16), NT, preferred_element_type=F32) for p in range(n_pages)],
        axis=1) * scale
    s_past = s_past + cq - jnp.concatenate([ck] * T, axis=0)
    qf = qbd.astype(F32)
    s_new = []
    for j in range(T):
        kj = _bf_round(kn_ref[0, j:j + 1, :])
        sj = jnp.sum(qf * kj, axis=-1, keepdims=True) * scale
        ckj = jnp.concatenate([cn[:, j:j + 1]] * T, axis=0)
        s_new.append(jnp.where((r_iota // H) >= j, sj + cq - ckj, NEG))
    m = jnp.max(s_past, axis=-1, keepdims=True)
    for sj in s_new:
        m = jnp.maximum(m, sj)
    e_past = jnp.exp(s_past - m)
    e_new = [jnp.exp(sj - m) for sj in s_new]
    denom = jnp.sum(e_past, axis=-1, keepdims=True)
    for ej in e_new:
        denom = denom + ej
    p_past = (e_past / denom).astype(BF16)
    acc = jnp.zeros((R, FOX_W), F32)
    for p in range(n_pages):
        acc = acc + jnp.dot(p_past[:, p * PAGE_SIZE:(p + 1) * PAGE_SIZE], v_refs[p][0].astype(BF16),
                            preferred_element_type=F32)
    for j in range(T):
        pj = jnp.where((r_iota // H) >= j, e_new[j] / denom, 0.0)
        acc = acc + _bf_round(pj) * _bf_round(vn_ref[0, j:j + 1, :])
    acc = jnp.where(headmask, acc, 0.0)
    for t in range(T):
        o_ref[0, t:t + 1, :] = jnp.sum(acc[t * H:(t + 1) * H], axis=0, keepdims=True)


def _page_spec(p, width):
    return pl.BlockSpec((1, PAGE_SIZE, width), lambda b, pt, p=p: (pt[b, p], 0, 0))


def fox_sample_attend(fq, fk_new, fv_new, ckT, cnT, page_table, k_pool, v_pool):
    B, T, W = fq.shape
    n_pages = page_table.shape[1]
    past = n_pages * PAGE_SIZE
    per_b = lambda shape: pl.BlockSpec((1,) + shape, lambda b, pt: (b, 0, 0))
    grid_spec = pltpu.PrefetchScalarGridSpec(
        num_scalar_prefetch=1, grid=(B,),
        in_specs=[per_b((T, W)), per_b((T, W)), per_b((T, W)), per_b((FOX_HEADS, past)), per_b((FOX_HEADS, T))]
                 + [_page_spec(p, W) for p in range(n_pages)]
                 + [_page_spec(p, W) for p in range(n_pages)],
        out_specs=per_b((T, W)))
    return pl.pallas_call(
        functools.partial(_fox_sample_body, n_pages=n_pages),
        grid_spec=grid_spec,
        out_shape=jax.ShapeDtypeStruct((B, T, W), F32),
        compiler_params=pltpu.CompilerParams(dimension_semantics=("arbitrary",), vmem_limit_bytes=VMEM_LIMIT),
        name="fox_sample",
    )(page_table, fq, fk_new, fv_new, ckT, cnT, *([k_pool] * n_pages), *([v_pool] * n_pages))


def _nsa_sample_body(pt_ref, q_ref, gt_ref, ksn_ref, vsn_ref, kwn_ref, vwn_ref, wk_ref, wv_ref,
                     posk_ref, posv_ref, phik_ref, phiv_ref, *refs, n_pages, T):
    kc_refs = refs[:n_pages]
    vc_refs = refs[n_pages:2 * n_pages]
    ks_refs = refs[2 * n_pages:3 * n_pages]
    vs_refs = refs[3 * n_pages:4 * n_pages]
    o_ref = refs[4 * n_pages]
    past = n_pages * PAGE_SIZE
    win_buf = wk_ref.shape[1]
    scale = HEAD_DIM ** -0.5
    R = NSA_GROUP * T
    chunks = PAGE_SIZE // CMP_STRIDE

    def compress_pages(page_refs, pos_ref, phi_ref):
        first, second = [], []
        for p in range(n_pages):
            x = page_refs[p][0]
            first.append((x * pos_ref[0]).reshape(chunks, CMP_STRIDE, KV_W).sum(axis=1))
            second.append((x * pos_ref[1]).reshape(chunks, CMP_STRIDE, KV_W).sum(axis=1))
        first = jnp.concatenate(first, axis=0)
        second = jnp.concatenate(second, axis=0)
        pooled = first + pltpu.roll(second, first.shape[0] - 1, axis=0)
        return [jnp.dot(pooled[:, g * HEAD_DIM:(g + 1) * HEAD_DIM].astype(BF16), phi_ref[...],
                        preferred_element_type=F32).astype(BF16) for g in range(NSA_KV_HEADS)]

    k_cmp = compress_pages(kc_refs, posk_ref, phik_ref)
    v_cmp = compress_pages(vc_refs, posv_ref, phiv_ref)

    r_iota = lax.broadcasted_iota(jnp.int32, (R, 1), 0)
    t_of_r = r_iota % T
    tq = past + t_of_r
    n_iota = lax.broadcasted_iota(jnp.int32, (1, 128), 1)
    t_cmp = n_iota * CMP_STRIDE + (CMP_BLOCK - 1)
    dist_c = (tq - t_cmp).astype(F32)
    mask_c = t_cmp <= tq
    hs_r = lax.broadcasted_iota(jnp.int32, (128, R), 0)
    hs_c = lax.broadcasted_iota(jnp.int32, (128, R), 1)
    head_sum = ((hs_r < R) & (hs_c % T == hs_r % T)).astype(BF16)
    t_lane = past + lax.broadcasted_iota(jnp.int32, (1, 128), 1) % T
    n_slc = -(-(past + T) // SEL_BLOCK)
    n_slc_pad = -(-n_slc // 8) * 8
    new_blk = past // SEL_BLOCK
    s128 = lax.broadcasted_iota(jnp.int32, (128, 1), 0)
    kpos = lax.broadcasted_iota(jnp.int32, (1, past), 1)
    expand = ((kpos // SEL_BLOCK) == s128).astype(BF16)
    dist_s = (tq - kpos).astype(F32)
    tw = past - win_buf + lax.broadcasted_iota(jnp.int32, (1, win_buf), 1)
    dwin = tq - tw
    mask_w = (dwin >= 0) & (dwin < WINDOW) & (tw >= 0)
    dwin_f = dwin.astype(F32)

    def attend(s_past, mask_past, pv_past, new_scores, new_vals):
        s_past = jnp.where(mask_past, s_past, NEG)
        m = jnp.max(s_past, axis=-1, keepdims=True)
        for sj in new_scores:
            m = jnp.maximum(m, sj)
        e_past = jnp.where(mask_past, jnp.exp(s_past - m), 0.0)
        e_new = [jnp.where(sj > 0.5 * NEG, jnp.exp(sj - m), 0.0) for sj in new_scores]
        denom = jnp.sum(e_past, axis=-1, keepdims=True)
        for ej in e_new:
            denom = denom + ej
        out = pv_past((e_past / denom).astype(BF16))
        for ej, vj in zip(e_new, new_vals):
            out = out + _bf_round(ej / denom) * vj
        return out

    for g in range(NSA_KV_HEADS):
        cols = slice(g * HEAD_DIM, (g + 1) * HEAD_DIM)
        rows = slice(g * R, (g + 1) * R)
        slope = jnp.zeros((R, 1), F32)
        for h in range(NSA_GROUP):
            slope = jnp.where(r_iota // T == h, 2.0 ** -(g * NSA_GROUP + h + 1), slope)
        q = q_ref[0, rows, :].astype(BF16)
        qf = q.astype(F32)

        s = lax.dot_general(q, k_cmp[g], NT, preferred_element_type=F32) * scale - slope * dist_c
        p = _masked_softmax_rows(s, mask_c)
        o_cmp = jnp.dot(p.astype(BF16), v_cmp[g], preferred_element_type=F32)
        p_sum = sum(jnp.dot(head_sum, x, preferred_element_type=F32) for x in _split3(p))
        picked = _select_blocks(p_sum, t_lane, n_slc_pad, min(N_SEL, n_slc))[0:R]

        def new_keys(kn_ref, vn_ref, ok):
            scores, vals = [], []
            for j in range(T):
                kj = _bf_round(kn_ref[0, j:j + 1, cols])
                sj = jnp.sum(qf * kj, axis=-1, keepdims=True) * scale - slope * (t_of_r - j).astype(F32)
                scores.append(jnp.where(ok & (t_of_r >= j), sj, NEG))
                vals.append(_bf_round(vn_ref[0, j:j + 1, cols]))
            return scores, vals

        sel_past = jnp.dot(picked.astype(BF16), expand, preferred_element_type=F32) > 0.5
        s_past = jnp.concatenate(
            [lax.dot_general(q, ks_refs[p][0][:, cols].astype(BF16), NT, preferred_element_type=F32)
             for p in range(n_pages)], axis=1) * scale - slope * dist_s

        def sel_pv(pb):
            out = jnp.zeros((R, HEAD_DIM), F32)
            for p in range(n_pages):
                out = out + jnp.dot(pb[:, p * PAGE_SIZE:(p + 1) * PAGE_SIZE], vs_refs[p][0][:, cols].astype(BF16),
                                    preferred_element_type=F32)
            return out

        o_sel = attend(s_past, sel_past, sel_pv, *new_keys(ksn_ref, vsn_ref, picked[:, new_blk:new_blk + 1] > 0.5))

        s_w = lax.dot_general(q, wk_ref[0, :, cols].astype(BF16), NT, preferred_element_type=F32) * scale - slope * dwin_f
        o_win = attend(s_w, mask_w,
                       lambda pb: jnp.dot(pb, wv_ref[0, :, cols].astype(BF16), preferred_element_type=F32),
                       *new_keys(kwn_ref, vwn_ref, t_of_r >= 0))

        gts = gt_ref[0, rows, :]
        o_ref[0, rows, :] = gts[:, 0:1] * o_cmp + gts[:, 1:2] * o_sel + gts[:, 2:3] * o_win


def nsa_sample_attend(nq_ht, gates_ht, ks_new, vs_new, kw_new, vw_new, win_k, win_v, pos_k2, pos_v2, phi_k, phi_v,
                      page_table, ck_pool, cv_pool, sk_pool, sv_pool):
    B, RT, D = nq_ht.shape
    T = RT // NSA_HEADS
    n_pages = page_table.shape[1]
    win = win_k.shape[1]
    assert (n_pages * PAGE_SIZE + T) // CMP_STRIDE == n_pages * PAGE_SIZE // CMP_STRIDE and T <= SEL_BLOCK
    per_b = lambda shape: pl.BlockSpec((1,) + shape, lambda b, pt: (b, 0, 0))
    const = lambda shape: pl.BlockSpec(shape, lambda b, pt: (0,) * len(shape))
    grid_spec = pltpu.PrefetchScalarGridSpec(
        num_scalar_prefetch=1, grid=(B,),
        in_specs=[per_b((RT, D)), per_b((RT, 3))] + [per_b((T, KV_W))] * 4 + [per_b((win, KV_W))] * 2
                 + [const((2, PAGE_SIZE, KV_W))] * 2 + [const((HEAD_DIM, HEAD_DIM))] * 2
                 + [_page_spec(p, KV_W) for p in range(n_pages)] * 4,
        out_specs=per_b((RT, D)))
    pools = [ck_pool] * n_pages + [cv_pool] * n_pages + [sk_pool] * n_pages + [sv_pool] * n_pages
    return pl.pallas_call(
        functools.partial(_nsa_sample_body, n_pages=n_pages, T=T),
        grid_spec=grid_spec,
        out_shape=jax.ShapeDtypeStruct((B, RT, D), F32),
        compiler_params=pltpu.CompilerParams(dimension_semantics=("arbitrary",), vmem_limit_bytes=VMEM_LIMIT),
        name="nsa_sample",
    )(page_table, nq_ht, gates_ht, ks_new, vs_new, kw_new, vw_new, win_k, win_v, pos_k2, pos_v2, phi_k, phi_v, *pools)


def _tile_pos(w_pos):
    halves = w_pos.reshape(2, CMP_STRIDE, HEAD_DIM)
    return jnp.tile(halves, (1, PAGE_SIZE // CMP_STRIDE, NSA_KV_HEADS))


def mixers_sample(fq, fk, fv, lf, nq, gates, kc, vc, ks, vs, kw, vw, win_k, win_v, page_table,
                  fox_k_pool, fox_v_pool, fox_lf_pool, cmp_k_pool, cmp_v_pool, slc_k_pool, slc_v_pool,
                  cmp_pos_k, cmp_phi_k, cmp_pos_v, cmp_phi_v):
    B, T = fq.shape[:2]
    n_phys = fox_k_pool.shape[0]
    past = page_table.shape[1] * PAGE_SIZE
    lf_past = fox_lf_pool[page_table].reshape(B, past, FOX_HEADS)
    c = jnp.cumsum(jnp.concatenate([lf_past, lf], axis=1), axis=1)
    ckT = jnp.swapaxes(c[:, :past], 1, 2)
    cnT = jnp.swapaxes(c[:, past:], 1, 2)
    fox_o = fox_sample_attend(fq.reshape(B, T, FOX_W), fk.reshape(B, T, FOX_W), fv.reshape(B, T, FOX_W), ckT, cnT,
                              page_table, fox_k_pool.reshape(n_phys, PAGE_SIZE, FOX_W),
                              fox_v_pool.reshape(n_phys, PAGE_SIZE, FOX_W))
    ht = lambda a: jnp.swapaxes(a, 1, 2).reshape(B, NSA_HEADS * T, a.shape[-1])
    flat = lambda a: a.reshape(a.shape[0], a.shape[1], KV_W)
    nsa_ht = nsa_sample_attend(ht(nq), ht(gates), flat(ks), flat(vs), flat(kw), flat(vw), flat(win_k), flat(win_v),
                               _tile_pos(cmp_pos_k), _tile_pos(cmp_pos_v), cmp_phi_k.astype(BF16), cmp_phi_v.astype(BF16),
                               page_table, flat(cmp_k_pool), flat(cmp_v_pool), flat(slc_k_pool), flat(slc_v_pool))
    nsa_o = jnp.swapaxes(nsa_ht.reshape(B, NSA_HEADS, T, HEAD_DIM), 1, 2).reshape(B, T, NSA_W)
    return fox_o, nsa_o


def merge_mixers(fox_o, nsa_o, g_fox_out, g_nsa_out, w_out):
    y = jnp.concatenate([rmsnorm(fox_o, g_fox_out), rmsnorm(nsa_o, g_nsa_out)], axis=-1)
    return y @ w_out


def memory_kv(mem, g, w_mk, w_mv):
    B, M, _ = mem.shape
    mn = rmsnorm(mem, g)
    return ((mn @ w_mk).reshape(B, M, MEM_HEADS, HEAD_DIM), (mn @ w_mv).reshape(B, M, MEM_HEADS, HEAD_DIM))


def memory_attend(x, g, w_mq, w_mo, mk, mv):
    B, T, _ = x.shape
    q = (rmsnorm(x, g) @ w_mq).reshape(B, T, MEM_HEADS, HEAD_DIM)
    s = jnp.einsum('bthd,bmhd->bhtm', q, mk).astype(F32) * HEAD_DIM ** -0.5
    p = jax.nn.softmax(s, axis=-1).astype(mv.dtype)
    o = jnp.einsum('bhtm,bmhd->bthd', p, mv).reshape(B, T, MEM_W)
    return o @ w_mo


def _top16_rows(s):
    iota = lax.broadcasted_iota(jnp.int32, s.shape, 0)
    vals = []
    for _ in range(PEER_TOPK):
        m = jnp.max(s, axis=0, keepdims=True)
        idx = jnp.min(jnp.where(s == m, iota, PEER_KEYS), axis=0, keepdims=True)
        s = jnp.where(iota == idx, NEG_INF, s)
        vals.append(m)
    return s == NEG_INF, jnp.concatenate(vals, axis=0)


def _peer_scores_body(x_ref, g_ref, wq_ref, k1_ref, k2_ref,
                      xn_ref, a_ref, b_ref, e1_ref, e2_ref, tau_ref):
    h = pl.program_id(1)

    @pl.when(h == 0)
    def _():
        xf = x_ref[...]
        y = xf * lax.rsqrt(jnp.mean(xf * xf, axis=-1, keepdims=True) + EPS) * g_ref[...]
        xn_ref[...] = y.astype(BF16)

    qT = lax.dot_general(wq_ref[...], xn_ref[...], (((1,), (1,)), ((), ())), preferred_element_type=F32)
    half = qT.shape[0] // 2
    s1 = jnp.dot(k1_ref[...], qT[:half].astype(BF16), preferred_element_type=F32)
    s2 = jnp.dot(k2_ref[...], qT[half:].astype(BF16), preferred_element_type=F32)
    mem1, v1 = _top16_rows(s1)
    mem2, v2 = _top16_rows(s2)
    pieces = [v1[0:1] + v2]
    for r in range(1, 8):
        pieces.append(v1[r:r + 1] + v2[0:8])
    pieces.append(v1[8:16] + v2[0:1])
    c = jnp.concatenate(pieces, axis=0)
    rank = jnp.zeros(c.shape, F32)
    for j in range(c.shape[0]):
        rank = rank + jnp.where(c[j:j + 1] > c, 1.0, 0.0)
    tau = jnp.min(jnp.where(rank <= PEER_TOPK - 1, c, jnp.inf), axis=0, keepdims=True)
    z = jnp.sum(jnp.where(c >= tau, jnp.exp(c - c[0:1]), 0.0), axis=0, keepdims=True)
    a_ref[0] = jnp.where(mem1, s1, NEG_INF)
    b_ref[0] = jnp.where(mem2, s2, NEG_INF)
    e1_ref[0] = jnp.exp(s1 - v1[0:1]) / z
    e2_ref[0] = jnp.exp(s2 - v2[0:1])
    tau_ref[0] = tau


def peer_scores(x2, g, wqT_bf, k1_bf, k2_bf, tt):
    n, d = x2.shape
    H = PEER_HEADS
    qd = wqT_bf.shape[0] // H
    return pl.pallas_call(
        _peer_scores_body,
        grid=(n // tt, H),
        in_specs=[pl.BlockSpec((tt, d), lambda i, h: (i, 0)),
                  pl.BlockSpec((1, d), lambda i, h: (0, 0)),
                  pl.BlockSpec((qd, d), lambda i, h: (h, 0)),
                  pl.BlockSpec((PEER_KEYS, qd // 2), lambda i, h: (0, 0)),
                  pl.BlockSpec((PEER_KEYS, qd // 2), lambda i, h: (0, 0))],
        out_specs=[pl.BlockSpec((tt, d), lambda i, h: (i, 0))]
                  + [pl.BlockSpec((1, PEER_KEYS, tt), lambda i, h: (h, 0, i))] * 4
                  + [pl.BlockSpec((1, 1, tt), lambda i, h: (h, 0, i))],
        out_shape=[jax.ShapeDtypeStruct((n, d), BF16)]
                  + [jax.ShapeDtypeStruct((H, PEER_KEYS, n), F32)] * 4
                  + [jax.ShapeDtypeStruct((H, 1, n), F32)],
        compiler_params=pltpu.CompilerParams(dimension_semantics=("arbitrary", "arbitrary"),
                                             vmem_limit_bytes=VMEM_LIMIT),
        name="peer_scores",
    )(x2, g.reshape(1, d), wqT_bf, k1_bf, k2_bf)


def _gelu_exact(x):
    return 0.5 * x * (1.0 + lax.erf(x * (2.0 ** -0.5)))


def _peer_dense_body(xn_ref, u_ref, vt_ref, a_ref, b_ref, e1_ref, e2_ref, tau_ref, o_ref, acc_ref, *, te):
    j = pl.program_id(1)

    @pl.when(j == 0)
    def _():
        acc_ref[...] = jnp.zeros(acc_ref.shape, F32)

    hT = lax.dot_general(u_ref[...], xn_ref[...], (((1,), (1,)), ((), ())), preferred_element_type=F32)
    na = te // PEER_KEYS
    rows = []
    for al in range(na):
        a = j * na + al
        w = jnp.zeros((PEER_KEYS, hT.shape[1]), F32)
        for h in range(PEER_HEADS):
            val = a_ref[h, pl.ds(a, 1), :] + b_ref[h]
            gate = e1_ref[h, pl.ds(a, 1), :] * e2_ref[h]
            w = w + jnp.where(val >= tau_ref[h], gate, 0.0)
        act = _gelu_exact(hT[al * PEER_KEYS:(al + 1) * PEER_KEYS])
        rows.append((w * act).astype(BF16))
    pT = jnp.concatenate(rows, axis=0)
    acc_ref[...] += jnp.dot(vt_ref[...], pT, preferred_element_type=F32)

    @pl.when(j == pl.num_programs(1) - 1)
    def _():
        o_ref[...] = acc_ref[...].T


def peer_dense(xn_bf, u_bf, vT_bf, a, b, e1, e2, tau, tt, te):
    n, d = xn_bf.shape
    ne = u_bf.shape[0]
    H = PEER_HEADS
    return pl.pallas_call(
        functools.partial(_peer_dense_body, te=te),
        grid=(n // tt, ne // te),
        in_specs=[pl.BlockSpec((tt, d), lambda i, j: (i, 0)),
                  pl.BlockSpec((te, d), lambda i, j: (j, 0)),
                  pl.BlockSpec((d, te), lambda i, j: (0, j))]
                 + [pl.BlockSpec((H, PEER_KEYS, tt), lambda i, j: (0, 0, i))] * 4
                 + [pl.BlockSpec((H, 1, tt), lambda i, j: (0, 0, i))],
        out_specs=pl.BlockSpec((tt, d), lambda i, j: (i, 0)),
        out_shape=jax.ShapeDtypeStruct((n, d), F32),
        scratch_shapes=[pltpu.VMEM((d, tt), F32)],
        compiler_params=pltpu.CompilerParams(dimension_semantics=("arbitrary", "arbitrary"),
                                             vmem_limit_bytes=VMEM_LIMIT),
        name="peer_dense",
    )(xn_bf, u_bf, vT_bf, a, b, e1, e2, tau)


def peer_ffn(x, g, wqT_bf, k1_bf, k2_bf, u_bf, vT_bf):
    B, T, D = x.shape
    x2 = x.reshape(B * T, D)
    xn_bf, a, b, e1, e2, tau = peer_scores(x2, g, wqT_bf, k1_bf, k2_bf, PEER_TOKEN_TILE)
    out = peer_dense(xn_bf, u_bf, vT_bf, a, b, e1, e2, tau, PEER_TOKEN_TILE, PEER_EXPERT_TILE)
    return out.reshape(B, T, D)


def kernel(x_prompt, x_sample, mem_prompt, cache_fox_k, cache_fox_v, cache_fox_logf,
           cache_cmp_k, cache_cmp_v, cache_slc_k, cache_slc_v, state_win_k, state_win_v,
           cache_mem_k, cache_mem_v, page_table,
           g_mix, w_in, b_fox_f, b_nsa_gate, cmp_pos_k, cmp_phi_k, cmp_pos_v, cmp_phi_v,
           g_fox_out, g_nsa_out, w_out, g_mem_q, g_mem_kv, w_mq, w_mk, w_mv, w_mo,
           g_peer, w_pq, peer_subkey_1, peer_subkey_2, peer_u, peer_v, g_final):
    depth = g_mix.shape[0]
    win_buf = min(WINDOW, PAST_LEN)
    names_p = ('fox_k', 'fox_v', 'fox_logf', 'cmp_k', 'cmp_v', 'slc_k', 'slc_v', 'win_k', 'win_v', 'mem_k', 'mem_v')
    names_s = ('fox_k', 'fox_v', 'fox_logf', 'cmp_k', 'cmp_v', 'slc_k', 'slc_v', 'win_k', 'win_v')
    sp = {n: [] for n in names_p}
    ss = {n: [] for n in names_s}
    xp, xs = x_prompt, x_sample
    for l in range(depth):
        fq, fk, fv, lf, nq, kc, vc, ks, vs, kw, vw, gt = project_mixers(xp, g_mix[l], w_in[l], b_fox_f[l], b_nsa_gate[l])
        fox_o = fox_prompt(fq, fk, fv, lf)
        nsa_o = nsa_prompt(nq, gt, kc, vc, ks, vs, kw, vw, cmp_pos_k[l], cmp_phi_k[l], cmp_pos_v[l], cmp_phi_v[l])
        xp = xp + merge_mixers(fox_o, nsa_o, g_fox_out[l], g_nsa_out[l], w_out[l])
        mk, mv = memory_kv(mem_prompt, g_mem_kv[l], w_mk[l], w_mv[l])
        xp = xp + memory_attend(xp, g_mem_q[l], w_mq[l], w_mo[l], mk, mv)
        peer_w = (w_pq[l].T.astype(BF16), peer_subkey_1[l].astype(BF16), peer_subkey_2[l].astype(BF16),
                  peer_u[l].astype(BF16), peer_v[l].T.astype(BF16))
        xp = xp + peer_ffn(xp, g_peer[l], *peer_w)
        for name, val in zip(names_p, (fk, fv, lf, kc, vc, ks, vs, last_rows(kw, win_buf), last_rows(vw, win_buf), mk, mv)):
            sp[name].append(val)

        fq, fk, fv, lf, nq, kc, vc, ks, vs, kw, vw, gt = project_mixers(xs, g_mix[l], w_in[l], b_fox_f[l], b_nsa_gate[l])
        kw_all = jnp.concatenate([state_win_k[l].astype(kw.dtype), kw], axis=1)
        vw_all = jnp.concatenate([state_win_v[l].astype(vw.dtype), vw], axis=1)
        fox_o, nsa_o = mixers_sample(fq, fk, fv, lf, nq, gt, kc, vc, ks, vs, kw, vw,
                                     state_win_k[l], state_win_v[l], page_table,
                                     cache_fox_k[l], cache_fox_v[l], cache_fox_logf[l],
                                     cache_cmp_k[l], cache_cmp_v[l], cache_slc_k[l], cache_slc_v[l],
                                     cmp_pos_k[l], cmp_phi_k[l], cmp_pos_v[l], cmp_phi_v[l])
        xs = xs + merge_mixers(fox_o, nsa_o, g_fox_out[l], g_nsa_out[l], w_out[l])
        xs = xs + memory_attend(xs, g_mem_q[l], w_mq[l], w_mo[l], cache_mem_k[l], cache_mem_v[l])
        xs = xs + peer_ffn(xs, g_peer[l], *peer_w)
        for name, val in zip(names_s, (fk, fv, lf, kc, vc, ks, vs, kw_all[:, -win_buf:], vw_all[:, -win_buf:])):
            ss[name].append(val)

    y_prompt = rmsnorm_pallas(xp, g_final)
    y_sample = rmsnorm_pallas(xs, g_final)
    sp = {n: jnp.stack(v) for n, v in sp.items()}
    ss = {n: jnp.stack(v) for n, v in ss.items()}
    return (y_prompt, y_sample,
            sp['fox_k'], sp['fox_v'], sp['fox_logf'], sp['cmp_k'], sp['cmp_v'], sp['slc_k'], sp['slc_v'],
            sp['win_k'], sp['win_v'], sp['mem_k'], sp['mem_v'],
            ss['fox_k'], ss['fox_v'], ss['fox_logf'], ss['cmp_k'], ss['cmp_v'], ss['slc_k'], ss['slc_v'],
            ss['win_k'], ss['win_v'])
```

```python
import functools
import math
import jax
import jax.numpy as jnp
from jax import lax
import numpy as np
from jax.experimental import pallas as pl
from jax.experimental.pallas import tpu as pltpu

D_MODEL = 2048
PAST_LEN = 2048
PAGE_SIZE = 128

HEAD_DIM = 128
FOX_HEADS = 8
NSA_HEADS = 8
NSA_KV_HEADS = 2
NSA_GROUP = NSA_HEADS // NSA_KV_HEADS
CMP_STRIDE = 16
CMP_BLOCK = 2 * CMP_STRIDE
SEL_BLOCK = 64
N_SEL = 16
WINDOW = 512
MEM_TOKENS = 256
MEM_HEADS = 4
PEER_HEADS = 8
PEER_KEYS = 128
PEER_EXPERTS = PEER_KEYS * PEER_KEYS
PEER_TOPK = 16
PEER_QDIM = 256
Q_BLOCK = 128
PEER_BLOCK = 128
FORGET_BIAS = 4.0
FORCE_BONUS = 1.0e4
NEG = -1.0e30
EPS = 1e-6

FOX_W = FOX_HEADS * HEAD_DIM
NSA_W = NSA_HEADS * HEAD_DIM
KV_W = NSA_KV_HEADS * HEAD_DIM
MIX_W = FOX_W + NSA_W
MEM_W = MEM_HEADS * HEAD_DIM
IN_SIZES = (FOX_W, FOX_W, FOX_W, FOX_HEADS, NSA_W, KV_W, KV_W, KV_W, KV_W, KV_W, KV_W, 3 * NSA_HEADS)
P_IN = sum(IN_SIZES)

F32 = jnp.float32
BF16 = jnp.bfloat16
NEG_INF = float("-inf")
VMEM_LIMIT = 56 * 1024 * 1024
PEER_TOKEN_TILE = 512
PEER_EXPERT_TILE = 1024
SLC_CHUNK = 512
FOX_CHUNK = 512
PROJ_ROW_TILE = 512
PROJ_COL_TILE = 512
PROJ_IN_COL_TILE = 640
MEM_ROW_TILE = 512
NT = (((1,), (1,)), ((), ()))


def rmsnorm(x, g):
    xf = x.astype(F32)
    y = xf * lax.rsqrt(jnp.mean(xf * xf, axis=-1, keepdims=True) + EPS)
    return (y * g.astype(F32)).astype(x.dtype)


def _rmsnorm_body(x_ref, g_ref, o_ref):
    xf = x_ref[...]
    y = xf * lax.rsqrt(jnp.mean(xf * xf, axis=-1, keepdims=True) + EPS)
    o_ref[...] = y * g_ref[...]


def rmsnorm_pallas(x, g, rows=256):
    shp = x.shape
    d = shp[-1]
    x2 = x.reshape(-1, d)
    n = x2.shape[0]
    rows = min(rows, n)
    assert n % rows == 0
    out = pl.pallas_call(
        _rmsnorm_body,
        grid=(n // rows,),
        in_specs=[pl.BlockSpec((rows, d), lambda i: (i, 0)),
                  pl.BlockSpec((1, d), lambda i: (0, 0))],
        out_specs=pl.BlockSpec((rows, d), lambda i: (i, 0)),
        out_shape=jax.ShapeDtypeStruct((n, d), F32),
        name="final_rmsnorm",
    )(x2, g.reshape(1, d))
    return out.reshape(shp)


def last_rows(a, n):
    T = a.shape[1]
    if T < n:
        a = jnp.pad(a, ((0, 0), (n - T, 0)) + ((0, 0),) * (a.ndim - 2))
    return a[:, -n:]


def _proj_body(*refs, n_in, normed, has_res):
    n_refs_in = n_in * (3 if normed else 2) + (1 if has_res else 0)
    o_ref = refs[n_refs_in]
    xn_refs = refs[n_refs_in + 1:]
    per = 3 if normed else 2

    @pl.when(pl.program_id(1) == 0)
    def _():
        for k in range(n_in):
            x = refs[k * per][...]
            if normed:
                x = x * lax.rsqrt(jnp.mean(x * x, axis=-1, keepdims=True) + EPS) * refs[k * per + 1][...]
            xn_refs[k][...] = x.astype(BF16)

    acc = jnp.dot(xn_refs[0][...], refs[per - 1][...], preferred_element_type=F32)
    for k in range(1, n_in):
        acc = acc + jnp.dot(xn_refs[k][...], refs[k * per + per - 1][...], preferred_element_type=F32)
    if has_res:
        acc = acc + refs[n_in * per][...]
    o_ref[...] = acc


def project(inputs, res=None, tn=PROJ_COL_TILE):
    n = inputs[0][0].shape[0]
    N = inputs[0][2].shape[1]
    normed = inputs[0][1] is not None
    tm = min(PROJ_ROW_TILE, n)
    tn = min(tn, N)
    assert n % tm == 0 and N % tn == 0
    args, in_specs, scratch = [], [], []
    for x, g, w in inputs:
        d = x.shape[1]
        args.append(x)
        in_specs.append(pl.BlockSpec((tm, d), lambda i, j: (i, 0)))
        if normed:
            args.append(g.reshape(1, d))
            in_specs.append(pl.BlockSpec((1, d), lambda i, j: (0, 0)))
        args.append(w)
        in_specs.append(pl.BlockSpec((d, tn), lambda i, j: (0, j)))
        scratch.append(pltpu.VMEM((tm, d), BF16))
    if res is not None:
        args.append(res)
        in_specs.append(pl.BlockSpec((tm, tn), lambda i, j: (i, j)))
    return pl.pallas_call(
        functools.partial(_proj_body, n_in=len(inputs), normed=normed, has_res=res is not None),
        grid=(n // tm, N // tn),
        in_specs=in_specs,
        out_specs=pl.BlockSpec((tm, tn), lambda i, j: (i, j)),
        out_shape=jax.ShapeDtypeStruct((n, N), F32),
        scratch_shapes=scratch,
        compiler_params=pltpu.CompilerParams(dimension_semantics=("arbitrary", "arbitrary"),
                                             vmem_limit_bytes=VMEM_LIMIT),
        name="project",
    )(*args)


_IN_OFFS = [int(o) for o in np.cumsum((0,) + IN_SIZES)]
_IN_ORDER = (0, 1, 2, 4, 5, 6, 7, 8, 9, 10, 3, 11)
P_IN_PAD = -(-P_IN // PROJ_IN_COL_TILE) * PROJ_IN_COL_TILE


def regroup_w_in(w_in):
    cols = [w_in[:, _IN_OFFS[k]:_IN_OFFS[k + 1]] for k in _IN_ORDER]
    cols.append(jnp.zeros((w_in.shape[0], P_IN_PAD - P_IN), w_in.dtype))
    return jnp.concatenate(cols, axis=1).astype(BF16)


def project_mixers(x, g_mix, w_in_bf, b_fox_f, b_nsa_gate):
    B, T, D = x.shape
    y = project([(x.reshape(B * T, D), g_mix, w_in_bf)], tn=PROJ_IN_COL_TILE)
    parts, off = {}, 0
    for k in _IN_ORDER:
        parts[k] = y[:, off:off + IN_SIZES[k]]
        off += IN_SIZES[k]
    hd = lambda a: a.reshape(B, T, -1, HEAD_DIM)
    logf = jax.nn.log_sigmoid(parts[3] + b_fox_f.astype(F32)).reshape(B, T, FOX_HEADS)
    gates = jax.nn.sigmoid(parts[11] + b_nsa_gate.astype(F32)).reshape(B, T, NSA_HEADS, 3)
    return (hd(parts[0]), hd(parts[1]), hd(parts[2]), logf, hd(parts[4]), hd(parts[5]), hd(parts[6]),
            hd(parts[7]), hd(parts[8]), hd(parts[9]), hd(parts[10]), gates)


def _fox_prompt_body(q_ref, k_ref, v_ref, cq_ref, ck_ref, o_ref):
    i = pl.program_id(2)
    h = pl.program_id(1)
    scale = HEAD_DIM ** -0.5
    q = q_ref[0].astype(BF16)
    t_row = i * Q_BLOCK + lax.broadcasted_iota(jnp.int32, (Q_BLOCK, 1), 0)
    h_lane = lax.broadcasted_iota(jnp.int32, (1, FOX_HEADS), 1)
    cq = jnp.sum(jnp.where(h_lane == h, cq_ref[0], 0.0), axis=-1, keepdims=True)
    n_chunks = (i * Q_BLOCK + Q_BLOCK + FOX_CHUNK - 1) // FOX_CHUNK

    def chunk(c, carry):
        m, l, acc = carry
        k0 = pl.multiple_of(c * FOX_CHUNK, FOX_CHUNK)
        kpos = k0 + lax.broadcasted_iota(jnp.int32, (1, FOX_CHUNK), 1)
        mask = kpos <= t_row
        s = lax.dot_general(q, k_ref[0, pl.ds(k0, FOX_CHUNK), :], NT, preferred_element_type=F32) * scale
        s = jnp.where(mask, s + cq - ck_ref[0, :, pl.ds(k0, FOX_CHUNK)], NEG)
        m_new = jnp.maximum(m, jnp.max(s, axis=-1, keepdims=True))
        alpha = jnp.exp(m - m_new)
        p = jnp.where(mask, jnp.exp(s - m_new), 0.0)
        l = alpha * l + jnp.sum(p, axis=-1, keepdims=True)
        acc = alpha * acc + jnp.dot(p.astype(BF16), v_ref[0, pl.ds(k0, FOX_CHUNK), :], preferred_element_type=F32)
        return m_new, l, acc

    init = (jnp.full((Q_BLOCK, 1), NEG, F32), jnp.zeros((Q_BLOCK, 1), F32), jnp.zeros((Q_BLOCK, HEAD_DIM), F32))
    _, l, acc = lax.fori_loop(0, n_chunks, chunk, init)
    o_ref[0] = acc / l


def fox_prompt(fq, fk, fv, logf):
    B, S = fq.shape[:2]
    assert S % FOX_CHUNK == 0
    H = FOX_HEADS
    c = jnp.cumsum(logf, axis=1)
    cT = jnp.swapaxes(c, 1, 2).reshape(B * H, 1, S)
    kv = lambda a: a.reshape(B, S, FOX_W).astype(BF16)
    head_cols = lambda rows: pl.BlockSpec((1, rows, HEAD_DIM), lambda b, h, i: (b, 0, h))
    return pl.pallas_call(
        _fox_prompt_body,
        grid=(B, H, S // Q_BLOCK),
        in_specs=[pl.BlockSpec((1, Q_BLOCK, HEAD_DIM), lambda b, h, i: (b, i, h)),
                  head_cols(S), head_cols(S),
                  pl.BlockSpec((1, Q_BLOCK, H), lambda b, h, i: (b, i, 0)),
                  pl.BlockSpec((1, 1, S), lambda b, h, i: (b * H + h, 0, 0))],
        out_specs=pl.BlockSpec((1, Q_BLOCK, HEAD_DIM), lambda b, h, i: (b, i, h)),
        out_shape=jax.ShapeDtypeStruct((B, S, FOX_W), F32),
        compiler_params=pltpu.CompilerParams(dimension_semantics=("arbitrary", "arbitrary", "arbitrary"),
                                             vmem_limit_bytes=VMEM_LIMIT),
        name="fox_prompt",
    )(fq.reshape(B, S, FOX_W), kv(fk), kv(fv), c, cT)


def compress(k, w_pos, w_phi):
    B, L, G, D = k.shape
    n_chunk = L // CMP_STRIDE
    c = k[:, :n_chunk * CMP_STRIDE].reshape(B, n_chunk, CMP_STRIDE, G, D)
    blocks = jnp.concatenate([c[:, :-1], c[:, 1:]], axis=2)
    pooled = jnp.einsum('bnlgd,ld->bngd', blocks, w_pos)
    return pooled @ w_phi


def cmp_positions(n_cmp):
    return jnp.arange(n_cmp, dtype=jnp.int32) * CMP_STRIDE + (CMP_BLOCK - 1)


def to_sel_blocks(k):
    B, L, G, D = k.shape
    n = -(-L // SEL_BLOCK)
    k = jnp.pad(k, ((0, 0), (0, n * SEL_BLOCK - L), (0, 0), (0, 0)))
    return k.reshape(B, n, SEL_BLOCK, G, D).transpose(0, 3, 1, 2, 4)


def nsa_attend(q, gates, tq, k_cmp, v_cmp, t_cmp, ks_blk, vs_blk, kw, vw, tw):
    B, T = q.shape[:2]
    G, HG = NSA_KV_HEADS, NSA_GROUP
    qg = q.reshape(B, T, G, HG, HEAD_DIM)
    slope = alibi_slopes(NSA_HEADS).reshape(G, HG)
    scale = HEAD_DIM ** -0.5

    s = jnp.einsum('btghd,bngd->btghn', qg, k_cmp).astype(F32) * scale
    dist = (tq[:, None] - t_cmp[None, :]).astype(F32)
    s = s - slope[None, None, :, :, None] * dist[None, :, None, None, :]
    p_cmp = masked_softmax(s, (t_cmp[None, :] <= tq[:, None])[None, :, None, None, :])
    o_cmp = jnp.einsum('btghn,bngd->btghd', p_cmp.astype(v_cmp.dtype), v_cmp)

    n_cmp, n_slc = k_cmp.shape[1], ks_blk.shape[2]
    jc = jnp.arange(n_cmp)[:, None] * CMP_STRIDE
    js = jnp.arange(n_slc)[None, :] * SEL_BLOCK
    sel_map = ((jc < js + SEL_BLOCK) & (jc + CMP_BLOCK > js)).astype(F32)
    imp = jnp.einsum('btghn,ns->btgs', p_cmp, sel_map)
    blk = jnp.arange(n_slc)[None, :]
    cur = (tq // SEL_BLOCK)[:, None]
    valid = (blk <= cur)[None, :, None, :]
    forced = ((blk == 0) | (blk == cur) | (blk == cur - 1))[None, :, None, :]
    score = jnp.where(valid, imp + jnp.where(forced, FORCE_BONUS, 0.0), NEG)
    top_s, idx = lax.top_k(score, min(N_SEL, n_slc))
    picked = top_s > 0.5 * NEG

    bi = jnp.arange(B)[:, None, None, None]
    gi = jnp.arange(G)[None, None, :, None]
    k_sel = ks_blk[bi, gi, idx]
    v_sel = vs_blk[bi, gi, idx]
    kpos = idx[..., None] * SEL_BLOCK + jnp.arange(SEL_BLOCK, dtype=jnp.int32)
    s = jnp.einsum('btghd,btgkld->btghkl', qg, k_sel).astype(F32) * scale
    dist = (tq[None, :, None, None, None] - kpos).astype(F32)
    s = s - slope[None, None, :, :, None, None] * dist[:, :, :, None]
    m = (picked[..., None] & (kpos <= tq[None, :, None, None, None]))[:, :, :, None]
    sf = s.reshape(B, T, G, HG, -1)
    p = masked_softmax(sf, jnp.broadcast_to(m, s.shape).reshape(sf.shape)).reshape(s.shape)
    o_sel = jnp.einsum('btghkl,btgkld->btghd', p.astype(v_sel.dtype), v_sel)

    s = jnp.einsum('btghd,bkgd->btghk', qg, kw).astype(F32) * scale
    dwin = tq[:, None] - tw[None, :]
    s = s - slope[None, None, :, :, None] * dwin.astype(F32)[None, :, None, None, :]
    mw = (dwin >= 0) & (dwin < WINDOW) & (tw[None, :] >= 0)
    p = masked_softmax(s, mw[None, :, None, None, :])
    o_win = jnp.einsum('btghk,bkgd->btghd', p.astype(vw.dtype), vw)

    g = gates.reshape(B, T, G, HG, 3)
    o = g[..., 0:1] * o_cmp + g[..., 1:2] * o_sel + g[..., 2:3] * o_win
    return o.reshape(B, T, NSA_W).astype(q.dtype)


def _masked_softmax_rows(s, mask):
    s = jnp.where(mask, s, NEG)
    e = jnp.exp(s - jnp.max(s, axis=-1, keepdims=True))
    p = e / jnp.sum(e, axis=-1, keepdims=True)
    return jnp.where(mask, p, 0.0)


def _split3(x):
    hi = x.astype(BF16)
    r = x - hi.astype(F32)
    mid = r.astype(BF16)
    lo = (r - mid.astype(F32)).astype(BF16)
    return hi, mid, lo


def _select_blocks(p_sum, t_lane, n_slc_pad, k_sel):
    T = p_sum.shape[0]
    s_iota = lax.broadcasted_iota(jnp.int32, (n_slc_pad, 1), 0)
    n_iota = lax.broadcasted_iota(jnp.int32, (1, p_sum.shape[1]), 1)
    ratio = SEL_BLOCK // CMP_STRIDE
    selT = ((n_iota < ratio * s_iota + ratio) & (n_iota > ratio * s_iota - CMP_BLOCK // CMP_STRIDE)).astype(BF16)
    impT = sum(lax.dot_general(selT, x, NT, preferred_element_type=F32) for x in _split3(p_sum))
    curT = t_lane // SEL_BLOCK
    validT = s_iota <= curT
    forcedT = (s_iota == 0) | (s_iota == curT) | (s_iota == curT - 1)
    scoreT = jnp.where(validT, impT + jnp.where(forcedT, FORCE_BONUS, 0.0), NEG)
    rank = jnp.zeros(scoreT.shape, F32)
    for sp in range(n_slc_pad):
        row = scoreT[sp:sp + 1]
        beats = (row > scoreT) | ((row == scoreT) & (s_iota > sp))
        rank = rank + jnp.where(beats, 1.0, 0.0)
    pickedT = jnp.where(validT & (rank < k_sel), 1.0, 0.0)
    pad = jnp.zeros((128 - n_slc_pad, T), F32)
    return jnp.concatenate([pickedT, pad], axis=0).T


def _nsa_prompt_body(q_ref, gt_ref, kcmp_ref, vcmp_ref, ks_ref, vs_ref, kw_ref, vw_ref, o_ref, *, n_slc):
    i = pl.program_id(1)
    scale = HEAD_DIM ** -0.5
    t_row = i * Q_BLOCK + lax.broadcasted_iota(jnp.int32, (Q_BLOCK, 1), 0)
    t_lane = i * Q_BLOCK + lax.broadcasted_iota(jnp.int32, (1, Q_BLOCK), 1)
    gts = gt_ref[0]
    n_iota = lax.broadcasted_iota(jnp.int32, (1, 128), 1)
    t_cmp = n_iota * CMP_STRIDE + (CMP_BLOCK - 1)
    dist_c = (t_row - t_cmp).astype(F32)
    mask_c = t_cmp <= t_row
    s128 = lax.broadcasted_iota(jnp.int32, (128, 1), 0)
    n_chunks = (i * Q_BLOCK + Q_BLOCK + SLC_CHUNK - 1) // SLC_CHUNK
    w0 = pl.multiple_of(jnp.maximum(i - WINDOW // Q_BLOCK, 0) * Q_BLOCK, Q_BLOCK)
    tw = w0 + lax.broadcasted_iota(jnp.int32, (1, WINDOW + Q_BLOCK), 1)
    dwin = t_row - tw
    mask_w = (dwin >= 0) & (dwin < WINDOW)
    dwin_f = dwin.astype(F32)

    for g in range(NSA_KV_HEADS):
        cols = slice(g * HEAD_DIM, (g + 1) * HEAD_DIM)
        heads = [g * NSA_GROUP + h for h in range(NSA_GROUP)]
        slopes = [2.0 ** -(hd + 1) for hd in heads]
        qs = [q_ref[0, :, hd * HEAD_DIM:(hd + 1) * HEAD_DIM].astype(BF16) for hd in heads]

        kc = kcmp_ref[0, :, cols]
        vc = vcmp_ref[0, :, cols]
        o_cmp = []
        p_sum = jnp.zeros((Q_BLOCK, 128), F32)
        for h in range(NSA_GROUP):
            s = lax.dot_general(qs[h], kc, NT, preferred_element_type=F32) * scale - slopes[h] * dist_c
            p = _masked_softmax_rows(s, mask_c)
            o_cmp.append(jnp.dot(p.astype(BF16), vc, preferred_element_type=F32))
            p_sum = p_sum + p

        picked_bf = _select_blocks(p_sum, t_lane, n_slc, min(N_SEL, n_slc)).astype(BF16)

        def chunk(c, carry):
            ms, ls, accs = carry
            k0 = pl.multiple_of(c * SLC_CHUNK, SLC_CHUNK)
            kb = ks_ref[0, pl.ds(k0, SLC_CHUNK), cols]
            vb = vs_ref[0, pl.ds(k0, SLC_CHUNK), cols]
            kpos = k0 + lax.broadcasted_iota(jnp.int32, (1, SLC_CHUNK), 1)
            expand = ((kpos // SEL_BLOCK) == s128).astype(BF16)
            mask = (jnp.dot(picked_bf, expand, preferred_element_type=F32) > 0.5) & (kpos <= t_row)
            dist = (t_row - kpos).astype(F32)
            new_m, new_l, new_acc = [], [], []
            for h in range(NSA_GROUP):
                s = lax.dot_general(qs[h], kb, NT, preferred_element_type=F32) * scale - slopes[h] * dist
                s = jnp.where(mask, s, NEG)
                m_new = jnp.maximum(ms[h], jnp.max(s, axis=-1, keepdims=True))
                alpha = jnp.exp(ms[h] - m_new)
                p = jnp.where(mask, jnp.exp(s - m_new), 0.0)
                new_l.append(alpha * ls[h] + jnp.sum(p, axis=-1, keepdims=True))
                new_acc.append(alpha * accs[h] + jnp.dot(p.astype(BF16), vb, preferred_element_type=F32))
                new_m.append(m_new)
            return tuple(new_m), tuple(new_l), tuple(new_acc)

        init = (tuple(jnp.full((Q_BLOCK, 1), NEG, F32) for _ in range(NSA_GROUP)),
                tuple(jnp.zeros((Q_BLOCK, 1), F32) for _ in range(NSA_GROUP)),
                tuple(jnp.zeros((Q_BLOCK, HEAD_DIM), F32) for _ in range(NSA_GROUP)))
        _, ls, accs = lax.fori_loop(0, n_chunks, chunk, init)

        kwb = kw_ref[0, pl.ds(w0, WINDOW + Q_BLOCK), cols]
        vwb = vw_ref[0, pl.ds(w0, WINDOW + Q_BLOCK), cols]
        for h in range(NSA_GROUP):
            s = lax.dot_general(qs[h], kwb, NT, preferred_element_type=F32) * scale - slopes[h] * dwin_f
            p = _masked_softmax_rows(s, mask_w)
            o_win = jnp.dot(p.astype(BF16), vwb, preferred_element_type=F32)
            hd = heads[h]
            o = (gts[:, 3 * hd:3 * hd + 1] * o_cmp[h] + gts[:, 3 * hd + 1:3 * hd + 2] * (accs[h] / ls[h])
                 + gts[:, 3 * hd + 2:3 * hd + 3] * o_win)
            o_ref[0, :, hd * HEAD_DIM:(hd + 1) * HEAD_DIM] = o


def nsa_prompt_attend(nq, gates, k_cmp, v_cmp, ks, vs, kw, vw):
    B, S, W = nq.shape
    assert S % SLC_CHUNK == 0 and S >= WINDOW + Q_BLOCK and S // SEL_BLOCK <= 128
    n_slc = S // SEL_BLOCK
    full = lambda: pl.BlockSpec((1, S, KV_W), lambda b, i: (b, 0, 0))
    return pl.pallas_call(
        functools.partial(_nsa_prompt_body, n_slc=n_slc),
        grid=(B, S // Q_BLOCK),
        in_specs=[pl.BlockSpec((1, Q_BLOCK, W), lambda b, i: (b, i, 0)),
                  pl.BlockSpec((1, Q_BLOCK, gates.shape[-1]), lambda b, i: (b, i, 0)),
                  pl.BlockSpec((1, 128, KV_W), lambda b, i: (b, 0, 0)),
                  pl.BlockSpec((1, 128, KV_W), lambda b, i: (b, 0, 0)),
                  full(), full(), full(), full()],
        out_specs=pl.BlockSpec((1, Q_BLOCK, W), lambda b, i: (b, i, 0)),
        out_shape=jax.ShapeDtypeStruct((B, S, W), F32),
        compiler_params=pltpu.CompilerParams(dimension_semantics=("arbitrary", "arbitrary"),
                                             vmem_limit_bytes=VMEM_LIMIT),
        name="nsa_prompt",
    )(nq, gates, k_cmp, v_cmp, ks, vs, kw, vw)


def _pad_cmp(c):
    B, n = c.shape[:2]
    return jnp.pad(c.reshape(B, n, KV_W), ((0, 0), (0, 128 - n), (0, 0))).astype(BF16)


def nsa_prompt(nq, gates, kc, vc, ks, vs, kw, vw, cmp_pos_k, cmp_phi_k, cmp_pos_v, cmp_phi_v):
    B, S = nq.shape[:2]
    flat = lambda a: a.reshape(B, S, KV_W).astype(BF16)
    k_cmp = _pad_cmp(compress(kc, cmp_pos_k, cmp_phi_k))
    v_cmp = _pad_cmp(compress(vc, cmp_pos_v, cmp_phi_v))
    return nsa_prompt_attend(nq.reshape(B, S, NSA_W), gates.reshape(B, S, 3 * NSA_HEADS), k_cmp, v_cmp,
                             flat(ks), flat(vs), flat(kw), flat(vw))


def _bf_round(x):
    return x.astype(BF16).astype(F32)


def _fox_sample_body(pt_ref, q_ref, kn_ref, vn_ref, ck_ref, cn_ref, cq_ref, *refs, n_pages):
    k_refs, v_refs, o_ref = refs[:n_pages], refs[n_pages:2 * n_pages], refs[2 * n_pages]
    R = q_ref.shape[1]
    H = FOX_HEADS
    scale = HEAD_DIM ** -0.5
    q = q_ref[0].astype(BF16)
    cq = cq_ref[0]
    r_iota = lax.broadcasted_iota(jnp.int32, (R, 1), 0)
    width = PAGE_SIZE * H
    same_head = (lax.broadcasted_iota(jnp.int32, (1, width), 1) % H) == (r_iota % H)
    segs = []
    for p in range(n_pages):
        s = lax.dot_general(q, k_refs[p][0].astype(BF16), NT, preferred_element_type=F32) * scale
        segs.append(jnp.where(same_head, s + cq - ck_ref[0, :, p * width:(p + 1) * width], NEG))
    l_new = lax.broadcasted_iota(jnp.int32, (1, kn_ref.shape[1]), 1)
    new_ok = ((l_new % H) == (r_iota % H)) & ((l_new // H) <= (r_iota // H))
    s = lax.dot_general(q, kn_ref[0].astype(BF16), NT, preferred_element_type=F32) * scale
    segs.append(jnp.where(new_ok, s + cq - cn_ref[0], NEG))
    m = functools.reduce(jnp.maximum, [jnp.max(s, axis=-1, keepdims=True) for s in segs])
    es = [jnp.exp(s - m) for s in segs]
    inv = 1.0 / functools.reduce(jnp.add, [jnp.sum(e, axis=-1, keepdims=True) for e in es])
    acc = jnp.dot((es[n_pages] * inv).astype(BF16), vn_ref[0].astype(BF16), preferred_element_type=F32)
    for p in range(n_pages):
        acc = acc + jnp.dot((es[p] * inv).astype(BF16), v_refs[p][0].astype(BF16), preferred_element_type=F32)
    o_ref[0] = acc


def _page_spec(p, rows):
    return pl.BlockSpec((1, rows, HEAD_DIM), lambda b, pt, p=p: (pt[b, p], 0, 0))


def fox_sample_attend(q_th, k_new, v_new, ck, cn, cq, page_table, k_pool, v_pool):
    B, R, D = q_th.shape
    n_pages = page_table.shape[1]
    rows = PAGE_SIZE * FOX_HEADS
    per_b = lambda shape: pl.BlockSpec((1,) + shape, lambda b, pt: (b, 0, 0))
    grid_spec = pltpu.PrefetchScalarGridSpec(
        num_scalar_prefetch=1, grid=(B,),
        in_specs=[per_b((R, D)), per_b(k_new.shape[1:]), per_b(v_new.shape[1:]), per_b(ck.shape[1:]),
                  per_b(cn.shape[1:]), per_b((R, 1))]
                 + [_page_spec(p, rows) for p in range(n_pages)]
                 + [_page_spec(p, rows) for p in range(n_pages)],
        out_specs=per_b((R, D)))
    return pl.pallas_call(
        functools.partial(_fox_sample_body, n_pages=n_pages),
        grid_spec=grid_spec,
        out_shape=jax.ShapeDtypeStruct((B, R, D), F32),
        compiler_params=pltpu.CompilerParams(dimension_semantics=("arbitrary",), vmem_limit_bytes=VMEM_LIMIT),
        name="fox_sample",
    )(page_table, q_th, k_new, v_new, ck, cn, cq, *([k_pool] * n_pages), *([v_pool] * n_pages))


def _nsa_sample_body(pt_ref, q_ref, gt_ref, ksn_ref, vsn_ref, kwn_ref, vwn_ref, wk_ref, wv_ref,
                     posk_ref, posv_ref, phik_ref, phiv_ref, *refs, n_pages, T):
    kc_refs = refs[:n_pages]
    vc_refs = refs[n_pages:2 * n_pages]
    ks_refs = refs[2 * n_pages:3 * n_pages]
    vs_refs = refs[3 * n_pages:4 * n_pages]
    o_ref = refs[4 * n_pages]
    G = NSA_KV_HEADS
    past = n_pages * PAGE_SIZE
    win_buf = wk_ref.shape[1] // G
    scale = HEAD_DIM ** -0.5
    R = NSA_GROUP * T
    chunks = PAGE_SIZE // CMP_STRIDE

    def group_rows(ref, g, n):
        return ref[pl.ds(0, 1), pl.ds(g, n, stride=G), :][0]

    def compress_pages(page_refs, pos_ref, phi_ref, g):
        first, second = [], []
        for p in range(n_pages):
            x = group_rows(page_refs[p], g, PAGE_SIZE)
            first.append((x * pos_ref[0]).reshape(chunks, CMP_STRIDE, HEAD_DIM).sum(axis=1))
            second.append((x * pos_ref[1]).reshape(chunks, CMP_STRIDE, HEAD_DIM).sum(axis=1))
        first = jnp.concatenate(first, axis=0)
        second = jnp.concatenate(second, axis=0)
        pooled = first + pltpu.roll(second, first.shape[0] - 1, axis=0)
        return jnp.dot(pooled.astype(BF16), phi_ref[...], preferred_element_type=F32).astype(BF16)

    r_iota = lax.broadcasted_iota(jnp.int32, (R, 1), 0)
    t_of_r = r_iota % T
    tq = past + t_of_r
    n_iota = lax.broadcasted_iota(jnp.int32, (1, 128), 1)
    t_cmp = n_iota * CMP_STRIDE + (CMP_BLOCK - 1)
    dist_c = (tq - t_cmp).astype(F32)
    mask_c = t_cmp <= tq
    hs_r = lax.broadcasted_iota(jnp.int32, (128, R), 0)
    hs_c = lax.broadcasted_iota(jnp.int32, (128, R), 1)
    head_sum = ((hs_r < R) & (hs_c % T == hs_r % T)).astype(BF16)
    t_lane = past + lax.broadcasted_iota(jnp.int32, (1, 128), 1) % T
    n_slc = -(-(past + T) // SEL_BLOCK)
    n_slc_pad = -(-n_slc // 8) * 8
    new_blk = past // SEL_BLOCK
    s128 = lax.broadcasted_iota(jnp.int32, (128, 1), 0)
    kpos = lax.broadcasted_iota(jnp.int32, (1, past), 1)
    expand = ((kpos // SEL_BLOCK) == s128).astype(BF16)
    dist_s = (tq - kpos).astype(F32)
    tw = past - win_buf + lax.broadcasted_iota(jnp.int32, (1, win_buf), 1)
    dwin = tq - tw
    mask_w = (dwin >= 0) & (dwin < WINDOW) & (tw >= 0)
    dwin_f = dwin.astype(F32)

    def attend(s_past, mask_past, pv_past, new_scores, new_vals):
        s_past = jnp.where(mask_past, s_past, NEG)
        m = jnp.max(s_past, axis=-1, keepdims=True)
        for sj in new_scores:
            m = jnp.maximum(m, sj)
        e_past = jnp.where(mask_past, jnp.exp(s_past - m), 0.0)
        e_new = [jnp.where(sj > 0.5 * NEG, jnp.exp(sj - m), 0.0) for sj in new_scores]
        denom = jnp.sum(e_past, axis=-1, keepdims=True)
        for ej in e_new:
            denom = denom + ej
        out = pv_past((e_past / denom).astype(BF16))
        for ej, vj in zip(e_new, new_vals):
            out = out + _bf_round(ej / denom) * vj
        return out

    for g in range(G):
        rows = slice(g * R, (g + 1) * R)
        slope = jnp.zeros((R, 1), F32)
        for h in range(NSA_GROUP):
            slope = jnp.where(r_iota // T == h, 2.0 ** -(g * NSA_GROUP + h + 1), slope)
        q = q_ref[0, rows, :].astype(BF16)
        qf = q.astype(F32)

        k_cmp = compress_pages(kc_refs, posk_ref, phik_ref, g)
        v_cmp = compress_pages(vc_refs, posv_ref, phiv_ref, g)
        s = lax.dot_general(q, k_cmp, NT, preferred_element_type=F32) * scale - slope * dist_c
        p = _masked_softmax_rows(s, mask_c)
        o_cmp = jnp.dot(p.astype(BF16), v_cmp, preferred_element_type=F32)
        p_sum = sum(jnp.dot(head_sum, x, preferred_element_type=F32) for x in _split3(p))
        picked = _select_blocks(p_sum, t_lane, n_slc_pad, min(N_SEL, n_slc))[0:R]

        def new_keys(kn_ref, vn_ref, ok):
            scores, vals = [], []
            for j in range(T):
                kj = _bf_round(kn_ref[0, j * G + g:j * G + g + 1, :])
                sj = jnp.sum(qf * kj, axis=-1, keepdims=True) * scale - slope * (t_of_r - j).astype(F32)
                scores.append(jnp.where(ok & (t_of_r >= j), sj, NEG))
                vals.append(_bf_round(vn_ref[0, j * G + g:j * G + g + 1, :]))
            return scores, vals

        sel_past = jnp.dot(picked.astype(BF16), expand, preferred_element_type=F32) > 0.5
        s_past = jnp.concatenate(
            [lax.dot_general(q, group_rows(ks_refs[p], g, PAGE_SIZE).astype(BF16), NT, preferred_element_type=F32)
             for p in range(n_pages)], axis=1) * scale - slope * dist_s

        def sel_pv(pb):
            out = jnp.zeros((R, HEAD_DIM), F32)
            for p in range(n_pages):
                out = out + jnp.dot(pb[:, p * PAGE_SIZE:(p + 1) * PAGE_SIZE],
                                    group_rows(vs_refs[p], g, PAGE_SIZE).astype(BF16), preferred_element_type=F32)
            return out

        o_sel = attend(s_past, sel_past, sel_pv, *new_keys(ksn_ref, vsn_ref, picked[:, new_blk:new_blk + 1] > 0.5))

        s_w = lax.dot_general(q, group_rows(wk_ref, g, win_buf).astype(BF16), NT,
                              preferred_element_type=F32) * scale - slope * dwin_f
        o_win = attend(s_w, mask_w,
                       lambda pb: jnp.dot(pb, group_rows(wv_ref, g, win_buf).astype(BF16), preferred_element_type=F32),
                       *new_keys(kwn_ref, vwn_ref, t_of_r >= 0))

        gts = gt_ref[0, rows, :]
        o_ref[0, rows, :] = gts[:, 0:1] * o_cmp + gts[:, 1:2] * o_sel + gts[:, 2:3] * o_win


def nsa_sample_attend(nq_ht, gates_ht, ks_new, vs_new, kw_new, vw_new, win_k, win_v, pos_k2, pos_v2, phi_k, phi_v,
                      page_table, ck_pool, cv_pool, sk_pool, sv_pool):
    B, RT, D = nq_ht.shape
    T = RT // NSA_HEADS
    G = NSA_KV_HEADS
    n_pages = page_table.shape[1]
    assert (n_pages * PAGE_SIZE + T) // CMP_STRIDE == n_pages * PAGE_SIZE // CMP_STRIDE and T <= SEL_BLOCK
    per_b = lambda shape: pl.BlockSpec((1,) + shape, lambda b, pt: (b, 0, 0))
    const = lambda shape: pl.BlockSpec(shape, lambda b, pt: (0,) * len(shape))
    grid_spec = pltpu.PrefetchScalarGridSpec(
        num_scalar_prefetch=1, grid=(B,),
        in_specs=[per_b((RT, D)), per_b((RT, 3))] + [per_b((T * G, D))] * 4 + [per_b(win_k.shape[1:])] * 2
                 + [const((2, PAGE_SIZE, D))] * 2 + [const((HEAD_DIM, HEAD_DIM))] * 2
                 + [_page_spec(p, PAGE_SIZE * G) for p in range(n_pages)] * 4,
        out_specs=per_b((RT, D)))
    pools = [ck_pool] * n_pages + [cv_pool] * n_pages + [sk_pool] * n_pages + [sv_pool] * n_pages
    return pl.pallas_call(
        functools.partial(_nsa_sample_body, n_pages=n_pages, T=T),
        grid_spec=grid_spec,
        out_shape=jax.ShapeDtypeStruct((B, RT, D), F32),
        compiler_params=pltpu.CompilerParams(dimension_semantics=("arbitrary",), vmem_limit_bytes=VMEM_LIMIT),
        name="nsa_sample",
    )(page_table, nq_ht, gates_ht, ks_new, vs_new, kw_new, vw_new, win_k, win_v, pos_k2, pos_v2, phi_k, phi_v, *pools)


def _tile_pos(w_pos):
    halves = w_pos.reshape(2, CMP_STRIDE, HEAD_DIM)
    return jnp.tile(halves, (1, PAGE_SIZE // CMP_STRIDE, 1))


def mixers_sample(fq, fk, fv, lf, nq, gates, kc, vc, ks, vs, kw, vw, win_k, win_v, page_table,
                  fox_k_pool, fox_v_pool, fox_lf_pool, cmp_k_pool, cmp_v_pool, slc_k_pool, slc_v_pool,
                  cmp_pos_k, cmp_phi_k, cmp_pos_v, cmp_phi_v):
    B, T = fq.shape[:2]
    n_phys = fox_k_pool.shape[0]
    past = page_table.shape[1] * PAGE_SIZE
    H = FOX_HEADS
    assert T * H <= HEAD_DIM
    lf_past = fox_lf_pool[page_table].reshape(B, past, H)
    c = jnp.cumsum(jnp.concatenate([lf_past, lf], axis=1), axis=1)
    c_new = c[:, past:].reshape(B, 1, T * H)
    pad_rows = lambda a: jnp.pad(a.reshape(B, T * H, HEAD_DIM), ((0, 0), (0, HEAD_DIM - T * H), (0, 0)))
    fox_o = fox_sample_attend(fq.reshape(B, T * H, HEAD_DIM), pad_rows(fk), pad_rows(fv),
                              c[:, :past].reshape(B, 1, past * H),
                              jnp.pad(c_new, ((0, 0), (0, 0), (0, HEAD_DIM - T * H))), c_new.reshape(B, T * H, 1),
                              page_table, fox_k_pool.reshape(n_phys, PAGE_SIZE * H, HEAD_DIM),
                              fox_v_pool.reshape(n_phys, PAGE_SIZE * H, HEAD_DIM)).reshape(B, T, FOX_W)
    ht = lambda a: jnp.swapaxes(a, 1, 2).reshape(B, NSA_HEADS * T, a.shape[-1])
    flat = lambda a: a.reshape(a.shape[0], a.shape[1] * NSA_KV_HEADS, HEAD_DIM)
    nsa_ht = nsa_sample_attend(ht(nq), ht(gates), flat(ks), flat(vs), flat(kw), flat(vw), flat(win_k), flat(win_v),
                               _tile_pos(cmp_pos_k), _tile_pos(cmp_pos_v), cmp_phi_k.astype(BF16), cmp_phi_v.astype(BF16),
                               page_table, flat(cmp_k_pool), flat(cmp_v_pool), flat(slc_k_pool), flat(slc_v_pool))
    nsa_o = jnp.swapaxes(nsa_ht.reshape(B, NSA_HEADS, T, HEAD_DIM), 1, 2).reshape(B, T, NSA_W)
    return fox_o, nsa_o


def merge_mixers(x, fox_o, nsa_o, g_fox_out, g_nsa_out, w_out_bf):
    B, T, D = x.shape
    n = B * T
    y = project([(fox_o.reshape(n, FOX_W), g_fox_out, w_out_bf[:FOX_W]),
                 (nsa_o.reshape(n, NSA_W), g_nsa_out, w_out_bf[FOX_W:])], res=x.reshape(n, D))
    return y.reshape(B, T, D)


def memory_kv(mem, g, w_mkv_bf):
    B, M, D = mem.shape
    kv = project([(mem.reshape(B * M, D), g, w_mkv_bf)])
    return (kv[:, :MEM_W].reshape(B, M, MEM_HEADS, HEAD_DIM), kv[:, MEM_W:].reshape(B, M, MEM_HEADS, HEAD_DIM))


def _mem_attend_body(q_ref, k_ref, v_ref, o_ref):
    H = MEM_HEADS
    q = q_ref[0].astype(BF16)
    s = lax.dot_general(q, k_ref[0].astype(BF16), NT, preferred_element_type=F32) * HEAD_DIM ** -0.5
    r_iota = lax.broadcasted_iota(jnp.int32, (s.shape[0], 1), 0)
    c_iota = lax.broadcasted_iota(jnp.int32, (1, s.shape[1]), 1)
    s = jnp.where((c_iota % H) == (r_iota % H), s, NEG)
    e = jnp.exp(s - jnp.max(s, axis=-1, keepdims=True))
    p = e / jnp.sum(e, axis=-1, keepdims=True)
    o_ref[0] = jnp.dot(p.astype(BF16), v_ref[0].astype(BF16), preferred_element_type=F32)


def memory_attend(x, g, w_mq_bf, w_mo_bf, mk, mv):
    B, T, D = x.shape
    M, H = mk.shape[1], MEM_HEADS
    x2 = x.reshape(B * T, D)
    q = project([(x2, g, w_mq_bf)]).reshape(B, T * H, HEAD_DIM)
    rows = min(MEM_ROW_TILE, T * H)
    kv_spec = pl.BlockSpec((1, M * H, HEAD_DIM), lambda b, i: (b, 0, 0))
    o = pl.pallas_call(
        _mem_attend_body,
        grid=(B, T * H // rows),
        in_specs=[pl.BlockSpec((1, rows, HEAD_DIM), lambda b, i: (b, i, 0)), kv_spec, kv_spec],
        out_specs=pl.BlockSpec((1, rows, HEAD_DIM), lambda b, i: (b, i, 0)),
        out_shape=jax.ShapeDtypeStruct((B, T * H, HEAD_DIM), F32),
        compiler_params=pltpu.CompilerParams(dimension_semantics=("arbitrary", "arbitrary"),
                                             vmem_limit_bytes=VMEM_LIMIT),
        name="mem_attend",
    )(q, mk.reshape(B, M * H, HEAD_DIM), mv.reshape(B, M * H, HEAD_DIM))
    return project([(o.reshape(B * T, MEM_W), None, w_mo_bf)], res=x2).reshape(B, T, D)


def _top16_rows(s):
    iota = lax.broadcasted_iota(jnp.int32, s.shape, 0)
    vals = []
    for _ in range(PEER_TOPK):
        m = jnp.max(s, axis=0, keepdims=True)
        idx = jnp.min(jnp.where(s == m, iota, PEER_KEYS), axis=0, keepdims=True)
        s = jnp.where(iota == idx, NEG_INF, s)
        vals.append(m)
    return s == NEG_INF, jnp.concatenate(vals, axis=0)


def _peer_scores_body(x_ref, g_ref, wq_ref, k1_ref, k2_ref,
                      xn_ref, a_ref, b_ref, e1_ref, e2_ref, tau_ref):
    h = pl.program_id(1)

    @pl.when(h == 0)
    def _():
        xf = x_ref[...]
        y = xf * lax.rsqrt(jnp.mean(xf * xf, axis=-1, keepdims=True) + EPS) * g_ref[...]
        xn_ref[...] = y.astype(BF16)

    qT = lax.dot_general(wq_ref[...], xn_ref[...], (((1,), (1,)), ((), ())), preferred_element_type=F32)
    half = qT.shape[0] // 2
    s1 = jnp.dot(k1_ref[...], qT[:half].astype(BF16), preferred_element_type=F32)
    s2 = jnp.dot(k2_ref[...], qT[half:].astype(BF16), preferred_element_type=F32)
    mem1, v1 = _top16_rows(s1)
    mem2, v2 = _top16_rows(s2)
    pieces = [v1[0:1] + v2]
    for r in range(1, 8):
        pieces.append(v1[r:r + 1] + v2[0:8])
    pieces.append(v1[8:16] + v2[0:1])
    c = jnp.concatenate(pieces, axis=0)
    rank = jnp.zeros(c.shape, F32)
    for j in range(c.shape[0]):
        rank = rank + jnp.where(c[j:j + 1] > c, 1.0, 0.0)
    tau = jnp.min(jnp.where(rank <= PEER_TOPK - 1, c, jnp.inf), axis=0, keepdims=True)
    z = jnp.sum(jnp.where(c >= tau, jnp.exp(c - c[0:1]), 0.0), axis=0, keepdims=True)
    a_ref[0] = jnp.where(mem1, s1, NEG_INF)
    b_ref[0] = jnp.where(mem2, s2, NEG_INF)
    e1_ref[0] = jnp.exp(s1 - v1[0:1]) / z
    e2_ref[0] = jnp.exp(s2 - v2[0:1])
    tau_ref[0] = tau


def peer_scores(x2, g, wqT_bf, k1_bf, k2_bf, tt):
    n, d = x2.shape
    H = PEER_HEADS
    qd = wqT_bf.shape[0] // H
    return pl.pallas_call(
        _peer_scores_body,
        grid=(n // tt, H),
        in_specs=[pl.BlockSpec((tt, d), lambda i, h: (i, 0)),
                  pl.BlockSpec((1, d), lambda i, h: (0, 0)),
                  pl.BlockSpec((qd, d), lambda i, h: (h, 0)),
                  pl.BlockSpec((PEER_KEYS, qd // 2), lambda i, h: (0, 0)),
                  pl.BlockSpec((PEER_KEYS, qd // 2), lambda i, h: (0, 0))],
        out_specs=[pl.BlockSpec((tt, d), lambda i, h: (i, 0))]
                  + [pl.BlockSpec((1, PEER_KEYS, tt), lambda i, h: (h, 0, i))] * 4
                  + [pl.BlockSpec((1, 1, tt), lambda i, h: (h, 0, i))],
        out_shape=[jax.ShapeDtypeStruct((n, d), BF16)]
                  + [jax.ShapeDtypeStruct((H, PEER_KEYS, n), F32)] * 4
                  + [jax.ShapeDtypeStruct((H, 1, n), F32)],
        compiler_params=pltpu.CompilerParams(dimension_semantics=("arbitrary", "arbitrary"),
                                             vmem_limit_bytes=VMEM_LIMIT),
        name="peer_scores",
    )(x2, g.reshape(1, d), wqT_bf, k1_bf, k2_bf)


def _gelu_exact(x):
    return 0.5 * x * (1.0 + lax.erf(x * (2.0 ** -0.5)))


def _peer_dense_body(xn_ref, u_ref, vt_ref, a_ref, b_ref, e1_ref, e2_ref, tau_ref, o_ref, acc_ref, *, te):
    j = pl.program_id(1)

    @pl.when(j == 0)
    def _():
        acc_ref[...] = jnp.zeros(acc_ref.shape, F32)

    hT = lax.dot_general(u_ref[...], xn_ref[...], (((1,), (1,)), ((), ())), preferred_element_type=F32)
    na = te // PEER_KEYS
    rows = []
    for al in range(na):
        a = j * na + al
        w = jnp.zeros((PEER_KEYS, hT.shape[1]), F32)
        for h in range(PEER_HEADS):
            val = a_ref[h, pl.ds(a, 1), :] + b_ref[h]
            gate = e1_ref[h, pl.ds(a, 1), :] * e2_ref[h]
            w = w + jnp.where(val >= tau_ref[h], gate, 0.0)
        act = _gelu_exact(hT[al * PEER_KEYS:(al + 1) * PEER_KEYS])
        rows.append((w * act).astype(BF16))
    pT = jnp.concatenate(rows, axis=0)
    acc_ref[...] += jnp.dot(vt_ref[...], pT, preferred_element_type=F32)

    @pl.when(j == pl.num_programs(1) - 1)
    def _():
        o_ref[...] = acc_ref[...].T


def peer_dense(xn_bf, u_bf, vT_bf, a, b, e1, e2, tau, tt, te):
    n, d = xn_bf.shape
    ne = u_bf.shape[0]
    H = PEER_HEADS
    return pl.pallas_call(
        functools.partial(_peer_dense_body, te=te),
        grid=(n // tt, ne // te),
        in_specs=[pl.BlockSpec((tt, d), lambda i, j: (i, 0)),
                  pl.BlockSpec((te, d), lambda i, j: (j, 0)),
                  pl.BlockSpec((d, te), lambda i, j: (0, j))]
                 + [pl.BlockSpec((H, PEER_KEYS, tt), lambda i, j: (0, 0, i))] * 4
                 + [pl.BlockSpec((H, 1, tt), lambda i, j: (0, 0, i))],
        out_specs=pl.BlockSpec((tt, d), lambda i, j: (i, 0)),
        out_shape=jax.ShapeDtypeStruct((n, d), F32),
        scratch_shapes=[pltpu.VMEM((d, tt), F32)],
        compiler_params=pltpu.CompilerParams(dimension_semantics=("arbitrary", "arbitrary"),
                                             vmem_limit_bytes=VMEM_LIMIT),
        name="peer_dense",
    )(xn_bf, u_bf, vT_bf, a, b, e1, e2, tau)


def peer_ffn(x, g, wqT_bf, k1_bf, k2_bf, u_bf, vT_bf):
    B, T, D = x.shape
    x2 = x.reshape(B * T, D)
    tt = min(PEER_TOKEN_TILE, B * T)
    assert (B * T) % tt == 0
    xn_bf, a, b, e1, e2, tau = peer_scores(x2, g, wqT_bf, k1_bf, k2_bf, tt)
    out = peer_dense(xn_bf, u_bf, vT_bf, a, b, e1, e2, tau, tt, PEER_EXPERT_TILE)
    return out.reshape(B, T, D)


def kernel(x_prompt, x_sample, mem_prompt, cache_fox_k, cache_fox_v, cache_fox_logf,
           cache_cmp_k, cache_cmp_v, cache_slc_k, cache_slc_v, state_win_k, state_win_v,
           cache_mem_k, cache_mem_v, page_table,
           g_mix, w_in, b_fox_f, b_nsa_gate, cmp_pos_k, cmp_phi_k, cmp_pos_v, cmp_phi_v,
           g_fox_out, g_nsa_out, w_out, g_mem_q, g_mem_kv, w_mq, w_mk, w_mv, w_mo,
           g_peer, w_pq, peer_subkey_1, peer_subkey_2, peer_u, peer_v, g_final):
    depth = g_mix.shape[0]
    win_buf = min(WINDOW, PAST_LEN)
    names_p = ('fox_k', 'fox_v', 'fox_logf', 'cmp_k', 'cmp_v', 'slc_k', 'slc_v', 'win_k', 'win_v', 'mem_k', 'mem_v')
    names_s = ('fox_k', 'fox_v', 'fox_logf', 'cmp_k', 'cmp_v', 'slc_k', 'slc_v', 'win_k', 'win_v')
    sp = {n: [] for n in names_p}
    ss = {n: [] for n in names_s}
    xp, xs = x_prompt, x_sample
    for l in range(depth):
        w_in_bf = regroup_w_in(w_in[l])
        w_out_bf, w_mq_bf, w_mo_bf = w_out[l].astype(BF16), w_mq[l].astype(BF16), w_mo[l].astype(BF16)
        w_mkv_bf = jnp.concatenate([w_mk[l], w_mv[l]], axis=1).astype(BF16)
        peer_w = (w_pq[l].T.astype(BF16), peer_subkey_1[l].astype(BF16), peer_subkey_2[l].astype(BF16),
                  peer_u[l].astype(BF16), peer_v[l].T.astype(BF16))

        fq, fk, fv, lf, nq, kc, vc, ks, vs, kw, vw, gt = project_mixers(xp, g_mix[l], w_in_bf, b_fox_f[l], b_nsa_gate[l])
        fox_o = fox_prompt(fq, fk, fv, lf)
        nsa_o = nsa_prompt(nq, gt, kc, vc, ks, vs, kw, vw, cmp_pos_k[l], cmp_phi_k[l], cmp_pos_v[l], cmp_phi_v[l])
        xp = merge_mixers(xp, fox_o, nsa_o, g_fox_out[l], g_nsa_out[l], w_out_bf)
        mk, mv = memory_kv(mem_prompt, g_mem_kv[l], w_mkv_bf)
        xp = memory_attend(xp, g_mem_q[l], w_mq_bf, w_mo_bf, mk, mv)
        xp = xp + peer_ffn(xp, g_peer[l], *peer_w)
        for name, val in zip(names_p, (fk, fv, lf, kc, vc, ks, vs, last_rows(kw, win_buf), last_rows(vw, win_buf), mk, mv)):
            sp[name].append(val)

        fq, fk, fv, lf, nq, kc, vc, ks, vs, kw, vw, gt = project_mixers(xs, g_mix[l], w_in_bf, b_fox_f[l], b_nsa_gate[l])
        kw_all = jnp.concatenate([state_win_k[l].astype(kw.dtype), kw], axis=1)
        vw_all = jnp.concatenate([state_win_v[l].astype(vw.dtype), vw], axis=1)
        fox_o, nsa_o = mixers_sample(fq, fk, fv, lf, nq, gt, kc, vc, ks, vs, kw, vw,
                                     state_win_k[l], state_win_v[l], page_table,
                                     cache_fox_k[l], cache_fox_v[l], cache_fox_logf[l],
                                     cache_cmp_k[l], cache_cmp_v[l], cache_slc_k[l], cache_slc_v[l],
                                     cmp_pos_k[l], cmp_phi_k[l], cmp_pos_v[l], cmp_phi_v[l])
        xs = merge_mixers(xs, fox_o, nsa_o, g_fox_out[l], g_nsa_out[l], w_out_bf)
        xs = memory_attend(xs, g_mem_q[l], w_mq_bf, w_mo_bf, cache_mem_k[l], cache_mem_v[l])
        xs = xs + peer_ffn(xs, g_peer[l], *peer_w)
        for name, val in zip(names_s, (fk, fv, lf, kc, vc, ks, vs, kw_all[:, -win_buf:], vw_all[:, -win_buf:])):
            ss[name].append(val)

    y_prompt = rmsnorm_pallas(xp, g_final)
    y_sample = rmsnorm_pallas(xs, g_final)
    sp = {n: jnp.stack(v) for n, v in sp.items()}
    ss = {n: jnp.stack(v) for n, v in ss.items()}
    return (y_prompt, y_sample,
            sp['fox_k'], sp['fox_v'], sp['fox_logf'], sp['cmp_k'], sp['cmp_v'], sp['slc_k'], sp['slc_v'],
            sp['win_k'], sp['win_v'], sp['mem_k'], sp['mem_v'],
            ss['fox_k'], ss['fox_v'], ss['fox_logf'], ss['cmp_k'], ss['cmp_v'], ss['slc_k'], ss['slc_v'],
            ss['win_k'], ss['win_v'])
```

```python
import functools
import math
import jax
import jax.numpy as jnp
from jax import lax
import numpy as np
from jax.experimental import pallas as pl
from jax.experimental.pallas import tpu as pltpu

D_MODEL = 2048
PAST_LEN = 2048
PAGE_SIZE = 128

HEAD_DIM = 128
FOX_HEADS = 8
NSA_HEADS = 8
NSA_KV_HEADS = 2
NSA_GROUP = NSA_HEADS // NSA_KV_HEADS
CMP_STRIDE = 16
CMP_BLOCK = 2 * CMP_STRIDE
SEL_BLOCK = 64
N_SEL = 16
WINDOW = 512
MEM_TOKENS = 256
MEM_HEADS = 4
PEER_HEADS = 8
PEER_KEYS = 128
PEER_EXPERTS = PEER_KEYS * PEER_KEYS
PEER_TOPK = 16
PEER_QDIM = 256
Q_BLOCK = 128
PEER_BLOCK = 128
FORGET_BIAS = 4.0
FORCE_BONUS = 1.0e4
NEG = -1.0e30
EPS = 1e-6

FOX_W = FOX_HEADS * HEAD_DIM
NSA_W = NSA_HEADS * HEAD_DIM
KV_W = NSA_KV_HEADS * HEAD_DIM
MIX_W = FOX_W + NSA_W
MEM_W = MEM_HEADS * HEAD_DIM
IN_SIZES = (FOX_W, FOX_W, FOX_W, FOX_HEADS, NSA_W, KV_W, KV_W, KV_W, KV_W, KV_W, KV_W, 3 * NSA_HEADS)
P_IN = sum(IN_SIZES)

F32 = jnp.float32
BF16 = jnp.bfloat16
NEG_INF = float("-inf")
VMEM_LIMIT = 56 * 1024 * 1024
PEER_TOKEN_TILE = 512
PEER_EXPERT_TILE = 1024
PEER_SUBTILES = 2
SLC_CHUNK = 512
FOX_CHUNK = 512
PROJ_ROW_TILE = 512
PROJ_COL_TILE = 512
PROJ_IN_COL_TILE = 640
MEM_ROW_TILE = 512
NT = (((1,), (1,)), ((), ()))


def rmsnorm(x, g):
    xf = x.astype(F32)
    y = xf * lax.rsqrt(jnp.mean(xf * xf, axis=-1, keepdims=True) + EPS)
    return (y * g.astype(F32)).astype(x.dtype)


def _rmsnorm_body(x_ref, g_ref, o_ref):
    xf = x_ref[...]
    y = xf * lax.rsqrt(jnp.mean(xf * xf, axis=-1, keepdims=True) + EPS)
    o_ref[...] = y * g_ref[...]


def rmsnorm_pallas(x, g, rows=256):
    shp = x.shape
    d = shp[-1]
    x2 = x.reshape(-1, d)
    n = x2.shape[0]
    rows = min(rows, n)
    assert n % rows == 0
    out = pl.pallas_call(
        _rmsnorm_body,
        grid=(n // rows,),
        in_specs=[pl.BlockSpec((rows, d), lambda i: (i, 0)),
                  pl.BlockSpec((1, d), lambda i: (0, 0))],
        out_specs=pl.BlockSpec((rows, d), lambda i: (i, 0)),
        out_shape=jax.ShapeDtypeStruct((n, d), F32),
        name="final_rmsnorm",
    )(x2, g.reshape(1, d))
    return out.reshape(shp)


def last_rows(a, n):
    T = a.shape[1]
    if T < n:
        a = jnp.pad(a, ((0, 0), (n - T, 0)) + ((0, 0),) * (a.ndim - 2))
    return a[:, -n:]


def _proj_body(*refs, n_in, normed, has_res):
    n_refs_in = n_in * (3 if normed else 2) + (1 if has_res else 0)
    o_ref = refs[n_refs_in]
    xn_refs = refs[n_refs_in + 1:]
    per = 3 if normed else 2

    @pl.when(pl.program_id(1) == 0)
    def _():
        for k in range(n_in):
            x = refs[k * per][...]
            if normed:
                x = x * lax.rsqrt(jnp.mean(x * x, axis=-1, keepdims=True) + EPS) * refs[k * per + 1][...]
            xn_refs[k][...] = x.astype(BF16)

    acc = jnp.dot(xn_refs[0][...], refs[per - 1][...], preferred_element_type=F32)
    for k in range(1, n_in):
        acc = acc + jnp.dot(xn_refs[k][...], refs[k * per + per - 1][...], preferred_element_type=F32)
    if has_res:
        acc = acc + refs[n_in * per][...]
    o_ref[...] = acc


def project(inputs, res=None, tn=PROJ_COL_TILE):
    n = inputs[0][0].shape[0]
    N = inputs[0][2].shape[1]
    normed = inputs[0][1] is not None
    tm = min(PROJ_ROW_TILE, n)
    tn = min(tn, N)
    assert n % tm == 0 and N % tn == 0
    args, in_specs, scratch = [], [], []
    for x, g, w in inputs:
        d = x.shape[1]
        args.append(x)
        in_specs.append(pl.BlockSpec((tm, d), lambda i, j: (i, 0)))
        if normed:
            args.append(g.reshape(1, d))
            in_specs.append(pl.BlockSpec((1, d), lambda i, j: (0, 0)))
        args.append(w)
        in_specs.append(pl.BlockSpec((d, tn), lambda i, j: (0, j)))
        scratch.append(pltpu.VMEM((tm, d), BF16))
    if res is not None:
        args.append(res)
        in_specs.append(pl.BlockSpec((tm, tn), lambda i, j: (i, j)))
    return pl.pallas_call(
        functools.partial(_proj_body, n_in=len(inputs), normed=normed, has_res=res is not None),
        grid=(n // tm, N // tn),
        in_specs=in_specs,
        out_specs=pl.BlockSpec((tm, tn), lambda i, j: (i, j)),
        out_shape=jax.ShapeDtypeStruct((n, N), F32),
        scratch_shapes=scratch,
        compiler_params=pltpu.CompilerParams(dimension_semantics=("arbitrary", "arbitrary"),
                                             vmem_limit_bytes=VMEM_LIMIT),
        name="project",
    )(*args)


_IN_OFFS = [int(o) for o in np.cumsum((0,) + IN_SIZES)]
_IN_ORDER = (0, 1, 2, 4, 5, 6, 7, 8, 9, 10, 3, 11)
P_IN_PAD = -(-P_IN // PROJ_IN_COL_TILE) * PROJ_IN_COL_TILE


def regroup_w_in(w_in):
    cols = [w_in[:, _IN_OFFS[k]:_IN_OFFS[k + 1]] for k in _IN_ORDER]
    cols.append(jnp.zeros((w_in.shape[0], P_IN_PAD - P_IN), w_in.dtype))
    return jnp.concatenate(cols, axis=1).astype(BF16)


def project_mixers(x, g_mix, w_in_bf, b_fox_f, b_nsa_gate):
    B, T, D = x.shape
    y = project([(x.reshape(B * T, D), g_mix, w_in_bf)], tn=PROJ_IN_COL_TILE)
    parts, off = {}, 0
    for k in _IN_ORDER:
        parts[k] = y[:, off:off + IN_SIZES[k]]
        off += IN_SIZES[k]
    hd = lambda a: a.reshape(B, T, -1, HEAD_DIM)
    logf = jax.nn.log_sigmoid(parts[3] + b_fox_f.astype(F32)).reshape(B, T, FOX_HEADS)
    gates = jax.nn.sigmoid(parts[11] + b_nsa_gate.astype(F32)).reshape(B, T, NSA_HEADS, 3)
    return (hd(parts[0]), hd(parts[1]), hd(parts[2]), logf, hd(parts[4]), hd(parts[5]), hd(parts[6]),
            hd(parts[7]), hd(parts[8]), hd(parts[9]), hd(parts[10]), gates)


def _fox_prompt_body(q_ref, k_ref, v_ref, cq_ref, ck_ref, o_ref, m_ref, l_ref, acc_ref):
    i = pl.program_id(1)
    H = FOX_HEADS
    scale = HEAD_DIM ** -0.5
    t_row = i * Q_BLOCK + lax.broadcasted_iota(jnp.int32, (1, Q_BLOCK, 1), 1)
    n_chunks = (i * Q_BLOCK + Q_BLOCK + FOX_CHUNK - 1) // FOX_CHUNK
    head_cols = [slice(h * HEAD_DIM, (h + 1) * HEAD_DIM) for h in range(H)]
    qs = [q_ref[0, :, cols].astype(BF16) for cols in head_cols]
    cq_all = cq_ref[0]
    cq = jnp.stack([cq_all[:, h:h + 1] for h in range(H)])
    m_ref[...] = jnp.full(m_ref.shape, NEG, F32)
    l_ref[...] = jnp.zeros(l_ref.shape, F32)
    acc_ref[...] = jnp.zeros(acc_ref.shape, F32)

    def chunk(c, carry):
        k0 = pl.multiple_of(c * FOX_CHUNK, FOX_CHUNK)
        kpos = k0 + lax.broadcasted_iota(jnp.int32, (1, 1, FOX_CHUNK), 2)
        mask = kpos <= t_row
        s = jnp.stack([lax.dot_general(qs[h], k_ref[0, pl.ds(k0, FOX_CHUNK), head_cols[h]], NT,
                                       preferred_element_type=F32) for h in range(H)]) * scale
        s = jnp.where(mask, s + cq - ck_ref[0, :, :, pl.ds(k0, FOX_CHUNK)], NEG)
        m_old = m_ref[...]
        m_new = jnp.maximum(m_old, jnp.max(s, axis=-1, keepdims=True))
        alpha = jnp.exp(m_old - m_new)
        p = jnp.where(mask, jnp.exp(s - m_new), 0.0)
        m_ref[...] = m_new
        l_ref[...] = alpha * l_ref[...] + jnp.sum(p, axis=-1, keepdims=True)
        pb = p.astype(BF16)
        pv = jnp.stack([jnp.dot(pb[h], v_ref[0, pl.ds(k0, FOX_CHUNK), head_cols[h]], preferred_element_type=F32)
                        for h in range(H)])
        acc_ref[...] = alpha * acc_ref[...] + pv
        return carry

    lax.fori_loop(0, n_chunks, chunk, 0)
    o = acc_ref[...] / l_ref[...]
    for h in range(H):
        o_ref[0, :, head_cols[h]] = o[h]


def fox_prompt(fq, fk, fv, logf):
    B, S = fq.shape[:2]
    assert S % FOX_CHUNK == 0
    H = FOX_HEADS
    c = jnp.cumsum(logf, axis=1)
    kv = lambda a: a.reshape(B, S, FOX_W).astype(BF16)
    whole = lambda rows, width: pl.BlockSpec((1, rows, width), lambda b, i: (b, 0, 0))
    block = lambda width: pl.BlockSpec((1, Q_BLOCK, width), lambda b, i: (b, i, 0))
    return pl.pallas_call(
        _fox_prompt_body,
        grid=(B, S // Q_BLOCK),
        in_specs=[block(FOX_W), whole(S, FOX_W), whole(S, FOX_W), block(H),
                  pl.BlockSpec((1, H, 1, S), lambda b, i: (b, 0, 0, 0))],
        out_specs=block(FOX_W),
        out_shape=jax.ShapeDtypeStruct((B, S, FOX_W), F32),
        scratch_shapes=[pltpu.VMEM((H, Q_BLOCK, 1), F32), pltpu.VMEM((H, Q_BLOCK, 1), F32),
                        pltpu.VMEM((H, Q_BLOCK, HEAD_DIM), F32)],
        compiler_params=pltpu.CompilerParams(dimension_semantics=("arbitrary", "arbitrary"),
                                             vmem_limit_bytes=VMEM_LIMIT),
        name="fox_prompt",
    )(fq.reshape(B, S, FOX_W), kv(fk), kv(fv), c, jnp.swapaxes(c, 1, 2).reshape(B, H, 1, S))


def compress(k, w_pos, w_phi):
    B, L, G, D = k.shape
    n_chunk = L // CMP_STRIDE
    c = k[:, :n_chunk * CMP_STRIDE].reshape(B, n_chunk, CMP_STRIDE, G, D)
    blocks = jnp.concatenate([c[:, :-1], c[:, 1:]], axis=2)
    pooled = jnp.einsum('bnlgd,ld->bngd', blocks, w_pos)
    return pooled @ w_phi


def cmp_positions(n_cmp):
    return jnp.arange(n_cmp, dtype=jnp.int32) * CMP_STRIDE + (CMP_BLOCK - 1)


def to_sel_blocks(k):
    B, L, G, D = k.shape
    n = -(-L // SEL_BLOCK)
    k = jnp.pad(k, ((0, 0), (0, n * SEL_BLOCK - L), (0, 0), (0, 0)))
    return k.reshape(B, n, SEL_BLOCK, G, D).transpose(0, 3, 1, 2, 4)


def nsa_attend(q, gates, tq, k_cmp, v_cmp, t_cmp, ks_blk, vs_blk, kw, vw, tw):
    B, T = q.shape[:2]
    G, HG = NSA_KV_HEADS, NSA_GROUP
    qg = q.reshape(B, T, G, HG, HEAD_DIM)
    slope = alibi_slopes(NSA_HEADS).reshape(G, HG)
    scale = HEAD_DIM ** -0.5

    s = jnp.einsum('btghd,bngd->btghn', qg, k_cmp).astype(F32) * scale
    dist = (tq[:, None] - t_cmp[None, :]).astype(F32)
    s = s - slope[None, None, :, :, None] * dist[None, :, None, None, :]
    p_cmp = masked_softmax(s, (t_cmp[None, :] <= tq[:, None])[None, :, None, None, :])
    o_cmp = jnp.einsum('btghn,bngd->btghd', p_cmp.astype(v_cmp.dtype), v_cmp)

    n_cmp, n_slc = k_cmp.shape[1], ks_blk.shape[2]
    jc = jnp.arange(n_cmp)[:, None] * CMP_STRIDE
    js = jnp.arange(n_slc)[None, :] * SEL_BLOCK
    sel_map = ((jc < js + SEL_BLOCK) & (jc + CMP_BLOCK > js)).astype(F32)
    imp = jnp.einsum('btghn,ns->btgs', p_cmp, sel_map)
    blk = jnp.arange(n_slc)[None, :]
    cur = (tq // SEL_BLOCK)[:, None]
    valid = (blk <= cur)[None, :, None, :]
    forced = ((blk == 0) | (blk == cur) | (blk == cur - 1))[None, :, None, :]
    score = jnp.where(valid, imp + jnp.where(forced, FORCE_BONUS, 0.0), NEG)
    top_s, idx = lax.top_k(score, min(N_SEL, n_slc))
    picked = top_s > 0.5 * NEG

    bi = jnp.arange(B)[:, None, None, None]
    gi = jnp.arange(G)[None, None, :, None]
    k_sel = ks_blk[bi, gi, idx]
    v_sel = vs_blk[bi, gi, idx]
    kpos = idx[..., None] * SEL_BLOCK + jnp.arange(SEL_BLOCK, dtype=jnp.int32)
    s = jnp.einsum('btghd,btgkld->btghkl', qg, k_sel).astype(F32) * scale
    dist = (tq[None, :, None, None, None] - kpos).astype(F32)
    s = s - slope[None, None, :, :, None, None] * dist[:, :, :, None]
    m = (picked[..., None] & (kpos <= tq[None, :, None, None, None]))[:, :, :, None]
    sf = s.reshape(B, T, G, HG, -1)
    p = masked_softmax(sf, jnp.broadcast_to(m, s.shape).reshape(sf.shape)).reshape(s.shape)
    o_sel = jnp.einsum('btghkl,btgkld->btghd', p.astype(v_sel.dtype), v_sel)

    s = jnp.einsum('btghd,bkgd->btghk', qg, kw).astype(F32) * scale
    dwin = tq[:, None] - tw[None, :]
    s = s - slope[None, None, :, :, None] * dwin.astype(F32)[None, :, None, None, :]
    mw = (dwin >= 0) & (dwin < WINDOW) & (tw[None, :] >= 0)
    p = masked_softmax(s, mw[None, :, None, None, :])
    o_win = jnp.einsum('btghk,bkgd->btghd', p.astype(vw.dtype), vw)

    g = gates.reshape(B, T, G, HG, 3)
    o = g[..., 0:1] * o_cmp + g[..., 1:2] * o_sel + g[..., 2:3] * o_win
    return o.reshape(B, T, NSA_W).astype(q.dtype)


def _masked_softmax_rows(s, mask):
    s = jnp.where(mask, s, NEG)
    e = jnp.exp(s - jnp.max(s, axis=-1, keepdims=True))
    p = e / jnp.sum(e, axis=-1, keepdims=True)
    return jnp.where(mask, p, 0.0)


def _split3(x):
    hi = x.astype(BF16)
    r = x - hi.astype(F32)
    mid = r.astype(BF16)
    lo = (r - mid.astype(F32)).astype(BF16)
    return hi, mid, lo


def _select_blocks(p_sum, t_lane, n_slc_pad, k_sel):
    T = p_sum.shape[0]
    s_iota = lax.broadcasted_iota(jnp.int32, (n_slc_pad, 1), 0)
    n_iota = lax.broadcasted_iota(jnp.int32, (1, p_sum.shape[1]), 1)
    ratio = SEL_BLOCK // CMP_STRIDE
    selT = ((n_iota < ratio * s_iota + ratio) & (n_iota > ratio * s_iota - CMP_BLOCK // CMP_STRIDE)).astype(BF16)
    impT = sum(lax.dot_general(selT, x, NT, preferred_element_type=F32) for x in _split3(p_sum))
    curT = t_lane // SEL_BLOCK
    validT = s_iota <= curT
    forcedT = (s_iota == 0) | (s_iota == curT) | (s_iota == curT - 1)
    scoreT = jnp.where(validT, impT + jnp.where(forcedT, FORCE_BONUS, 0.0), NEG)
    rank = jnp.zeros(scoreT.shape, F32)
    for sp in range(n_slc_pad):
        row = scoreT[sp:sp + 1]
        beats = (row > scoreT) | ((row == scoreT) & (s_iota > sp))
        rank = rank + jnp.where(beats, 1.0, 0.0)
    pickedT = jnp.where(validT & (rank < k_sel), 1.0, 0.0)
    pad = jnp.zeros((128 - n_slc_pad, T), F32)
    return jnp.concatenate([pickedT, pad], axis=0).T


def _nsa_prompt_body(q_ref, gt_ref, kcmp_ref, vcmp_ref, ks_ref, vs_ref, kw_ref, vw_ref, expand_ref, o_ref,
                     m_ref, l_ref, acc_ref, *, n_slc):
    i = pl.program_id(1)
    scale = HEAD_DIM ** -0.5
    R = NSA_GROUP * Q_BLOCK
    r_iota = lax.broadcasted_iota(jnp.int32, (R, 1), 0)
    t_row = i * Q_BLOCK + r_iota % Q_BLOCK
    t_lane = i * Q_BLOCK + lax.broadcasted_iota(jnp.int32, (1, Q_BLOCK), 1)
    gts = gt_ref[0]
    n_iota = lax.broadcasted_iota(jnp.int32, (1, 128), 1)
    t_cmp = n_iota * CMP_STRIDE + (CMP_BLOCK - 1)
    dist_c = (t_row - t_cmp).astype(F32)
    mask_c = t_cmp <= t_row
    n_chunks = (i * Q_BLOCK + Q_BLOCK + SLC_CHUNK - 1) // SLC_CHUNK
    w0 = pl.multiple_of(jnp.maximum(i - WINDOW // Q_BLOCK, 0) * Q_BLOCK, Q_BLOCK)
    tw = w0 + lax.broadcasted_iota(jnp.int32, (1, WINDOW + Q_BLOCK), 1)
    dwin = t_row - tw
    mask_w = (dwin >= 0) & (dwin < WINDOW)
    dwin_f = dwin.astype(F32)

    for g in range(NSA_KV_HEADS):
        cols = slice(g * HEAD_DIM, (g + 1) * HEAD_DIM)
        heads = [g * NSA_GROUP + h for h in range(NSA_GROUP)]
        slope = jnp.zeros((R, 1), F32)
        for h, hd in enumerate(heads):
            slope = jnp.where(r_iota // Q_BLOCK == h, 2.0 ** -(hd + 1), slope)
        q = jnp.concatenate([q_ref[0, :, hd * HEAD_DIM:(hd + 1) * HEAD_DIM] for hd in heads], axis=0).astype(BF16)

        s = lax.dot_general(q, kcmp_ref[0, :, cols], NT, preferred_element_type=F32) * scale - slope * dist_c
        p = _masked_softmax_rows(s, mask_c)
        o_cmp = jnp.dot(p.astype(BF16), vcmp_ref[0, :, cols], preferred_element_type=F32)
        p_sum = functools.reduce(jnp.add, [p[h * Q_BLOCK:(h + 1) * Q_BLOCK] for h in range(NSA_GROUP)])
        picked = _select_blocks(p_sum, t_lane, n_slc, min(N_SEL, n_slc)).astype(BF16)
        picked_rows = jnp.concatenate([picked] * NSA_GROUP, axis=0)

        m_ref[...] = jnp.full(m_ref.shape, NEG, F32)
        l_ref[...] = jnp.zeros(l_ref.shape, F32)
        acc_ref[...] = jnp.zeros(acc_ref.shape, F32)

        def chunk(c, carry):
            k0 = pl.multiple_of(c * SLC_CHUNK, SLC_CHUNK)
            kpos = k0 + lax.broadcasted_iota(jnp.int32, (1, SLC_CHUNK), 1)
            sel = jnp.dot(picked_rows, expand_ref[:, pl.ds(k0, SLC_CHUNK)], preferred_element_type=F32)
            mask = (sel > 0.5) & (kpos <= t_row)
            s = lax.dot_general(q, ks_ref[0, pl.ds(k0, SLC_CHUNK), cols], NT, preferred_element_type=F32) * scale
            s = jnp.where(mask, s - slope * (t_row - kpos).astype(F32), NEG)
            m_old = m_ref[...]
            m_new = jnp.maximum(m_old, jnp.max(s, axis=-1, keepdims=True))
            alpha = jnp.exp(m_old - m_new)
            p = jnp.where(mask, jnp.exp(s - m_new), 0.0)
            m_ref[...] = m_new
            l_ref[...] = alpha * l_ref[...] + jnp.sum(p, axis=-1, keepdims=True)
            acc_ref[...] = alpha * acc_ref[...] + jnp.dot(p.astype(BF16), vs_ref[0, pl.ds(k0, SLC_CHUNK), cols],
                                                          preferred_element_type=F32)
            return carry

        lax.fori_loop(0, n_chunks, chunk, 0)
        o_sel = acc_ref[...] / l_ref[...]

        s = lax.dot_general(q, kw_ref[0, pl.ds(w0, WINDOW + Q_BLOCK), cols], NT,
                            preferred_element_type=F32) * scale - slope * dwin_f
        p = _masked_softmax_rows(s, mask_w)
        o_win = jnp.dot(p.astype(BF16), vw_ref[0, pl.ds(w0, WINDOW + Q_BLOCK), cols], preferred_element_type=F32)

        gate = lambda j: jnp.concatenate([gts[:, 3 * hd + j:3 * hd + j + 1] for hd in heads], axis=0)
        o = gate(0) * o_cmp + gate(1) * o_sel + gate(2) * o_win
        for h, hd in enumerate(heads):
            o_ref[0, :, hd * HEAD_DIM:(hd + 1) * HEAD_DIM] = o[h * Q_BLOCK:(h + 1) * Q_BLOCK]


def nsa_prompt_attend(nq, gates, k_cmp, v_cmp, ks, vs, kw, vw):
    B, S, W = nq.shape
    assert S % SLC_CHUNK == 0 and S >= WINDOW + Q_BLOCK and S // SEL_BLOCK <= 128
    n_slc = S // SEL_BLOCK
    expand = (jnp.arange(S)[None, :] // SEL_BLOCK == jnp.arange(128)[:, None]).astype(BF16)
    full = lambda: pl.BlockSpec((1, S, KV_W), lambda b, i: (b, 0, 0))
    rows = NSA_GROUP * Q_BLOCK
    return pl.pallas_call(
        functools.partial(_nsa_prompt_body, n_slc=n_slc),
        grid=(B, S // Q_BLOCK),
        in_specs=[pl.BlockSpec((1, Q_BLOCK, W), lambda b, i: (b, i, 0)),
                  pl.BlockSpec((1, Q_BLOCK, gates.shape[-1]), lambda b, i: (b, i, 0)),
                  pl.BlockSpec((1, 128, KV_W), lambda b, i: (b, 0, 0)),
                  pl.BlockSpec((1, 128, KV_W), lambda b, i: (b, 0, 0)),
                  full(), full(), full(), full(),
                  pl.BlockSpec((128, S), lambda b, i: (0, 0))],
        out_specs=pl.BlockSpec((1, Q_BLOCK, W), lambda b, i: (b, i, 0)),
        out_shape=jax.ShapeDtypeStruct((B, S, W), F32),
        scratch_shapes=[pltpu.VMEM((rows, 1), F32), pltpu.VMEM((rows, 1), F32), pltpu.VMEM((rows, HEAD_DIM), F32)],
        compiler_params=pltpu.CompilerParams(dimension_semantics=("arbitrary", "arbitrary"),
                                             vmem_limit_bytes=VMEM_LIMIT),
        name="nsa_prompt",
    )(nq, gates, k_cmp, v_cmp, ks, vs, kw, vw, expand)


def _pad_cmp(c):
    B, n = c.shape[:2]
    return jnp.pad(c.reshape(B, n, KV_W), ((0, 0), (0, 128 - n), (0, 0))).astype(BF16)


def nsa_prompt(nq, gates, kc, vc, ks, vs, kw, vw, cmp_pos_k, cmp_phi_k, cmp_pos_v, cmp_phi_v):
    B, S = nq.shape[:2]
    flat = lambda a: a.reshape(B, S, KV_W).astype(BF16)
    k_cmp = _pad_cmp(compress(kc, cmp_pos_k, cmp_phi_k))
    v_cmp = _pad_cmp(compress(vc, cmp_pos_v, cmp_phi_v))
    return nsa_prompt_attend(nq.reshape(B, S, NSA_W), gates.reshape(B, S, 3 * NSA_HEADS), k_cmp, v_cmp,
                             flat(ks), flat(vs), flat(kw), flat(vw))


def _bf_round(x):
    return x.astype(BF16).astype(F32)


def _fox_sample_body(pt_ref, q_ref, kn_ref, vn_ref, ck_ref, cn_ref, cq_ref, *refs, n_pages):
    k_refs, v_refs, o_ref = refs[:n_pages], refs[n_pages:2 * n_pages], refs[2 * n_pages]
    R = q_ref.shape[1]
    H = FOX_HEADS
    scale = HEAD_DIM ** -0.5
    q = q_ref[0].astype(BF16)
    cq = cq_ref[0]
    r_iota = lax.broadcasted_iota(jnp.int32, (R, 1), 0)
    width = PAGE_SIZE * H
    same_head = (lax.broadcasted_iota(jnp.int32, (1, width), 1) % H) == (r_iota % H)
    segs = []
    for p in range(n_pages):
        s = lax.dot_general(q, k_refs[p][0].astype(BF16), NT, preferred_element_type=F32) * scale
        segs.append(jnp.where(same_head, s + cq - ck_ref[0, :, p * width:(p + 1) * width], NEG))
    l_new = lax.broadcasted_iota(jnp.int32, (1, kn_ref.shape[1]), 1)
    new_ok = ((l_new % H) == (r_iota % H)) & ((l_new // H) <= (r_iota // H))
    s = lax.dot_general(q, kn_ref[0].astype(BF16), NT, preferred_element_type=F32) * scale
    segs.append(jnp.where(new_ok, s + cq - cn_ref[0], NEG))
    m = functools.reduce(jnp.maximum, [jnp.max(s, axis=-1, keepdims=True) for s in segs])
    es = [jnp.exp(s - m) for s in segs]
    inv = 1.0 / functools.reduce(jnp.add, [jnp.sum(e, axis=-1, keepdims=True) for e in es])
    acc = jnp.dot((es[n_pages] * inv).astype(BF16), vn_ref[0].astype(BF16), preferred_element_type=F32)
    for p in range(n_pages):
        acc = acc + jnp.dot((es[p] * inv).astype(BF16), v_refs[p][0].astype(BF16), preferred_element_type=F32)
    o_ref[0] = acc


def _page_spec(p, rows):
    return pl.BlockSpec((1, rows, HEAD_DIM), lambda b, pt, p=p: (pt[b, p], 0, 0))


def fox_sample_attend(q_th, k_new, v_new, ck, cn, cq, page_table, k_pool, v_pool):
    B, R, D = q_th.shape
    n_pages = page_table.shape[1]
    rows = PAGE_SIZE * FOX_HEADS
    per_b = lambda shape: pl.BlockSpec((1,) + shape, lambda b, pt: (b, 0, 0))
    grid_spec = pltpu.PrefetchScalarGridSpec(
        num_scalar_prefetch=1, grid=(B,),
        in_specs=[per_b((R, D)), per_b(k_new.shape[1:]), per_b(v_new.shape[1:]), per_b(ck.shape[1:]),
                  per_b(cn.shape[1:]), per_b((R, 1))]
                 + [_page_spec(p, rows) for p in range(n_pages)]
                 + [_page_spec(p, rows) for p in range(n_pages)],
        out_specs=per_b((R, D)))
    return pl.pallas_call(
        functools.partial(_fox_sample_body, n_pages=n_pages),
        grid_spec=grid_spec,
        out_shape=jax.ShapeDtypeStruct((B, R, D), F32),
        compiler_params=pltpu.CompilerParams(dimension_semantics=("arbitrary",), vmem_limit_bytes=VMEM_LIMIT),
        name="fox_sample",
    )(page_table, q_th, k_new, v_new, ck, cn, cq, *([k_pool] * n_pages), *([v_pool] * n_pages))


def _nsa_sample_body(pt_ref, q_ref, gt_ref, ksn_ref, vsn_ref, kwn_ref, vwn_ref, wk_ref, wv_ref,
                     posk_ref, posv_ref, phik_ref, phiv_ref, expand_ref, *refs, n_pages, T):
    kc_refs = refs[:n_pages]
    vc_refs = refs[n_pages:2 * n_pages]
    ks_refs = refs[2 * n_pages:3 * n_pages]
    vs_refs = refs[3 * n_pages:4 * n_pages]
    o_ref = refs[4 * n_pages]
    G = NSA_KV_HEADS
    past = n_pages * PAGE_SIZE
    win_buf = wk_ref.shape[1] // G
    scale = HEAD_DIM ** -0.5
    R = NSA_GROUP * T
    chunks = PAGE_SIZE // CMP_STRIDE

    def group_rows(ref, g, n):
        return ref[pl.ds(0, 1), pl.ds(g, n, stride=G), :][0]

    def compress_pages(page_refs, pos_ref, phi_ref, g):
        first, second = [], []
        for p in range(n_pages):
            x = group_rows(page_refs[p], g, PAGE_SIZE)
            first.append((x * pos_ref[0]).reshape(chunks, CMP_STRIDE, HEAD_DIM).sum(axis=1))
            second.append((x * pos_ref[1]).reshape(chunks, CMP_STRIDE, HEAD_DIM).sum(axis=1))
        first = jnp.concatenate(first, axis=0)
        second = jnp.concatenate(second, axis=0)
        pooled = first + pltpu.roll(second, first.shape[0] - 1, axis=0)
        return jnp.dot(pooled.astype(BF16), phi_ref[...], preferred_element_type=F32).astype(BF16)

    r_iota = lax.broadcasted_iota(jnp.int32, (R, 1), 0)
    t_of_r = r_iota % T
    tq = past + t_of_r
    n_iota = lax.broadcasted_iota(jnp.int32, (1, 128), 1)
    t_cmp = n_iota * CMP_STRIDE + (CMP_BLOCK - 1)
    dist_c = (tq - t_cmp).astype(F32)
    mask_c = t_cmp <= tq
    hs_r = lax.broadcasted_iota(jnp.int32, (128, R), 0)
    hs_c = lax.broadcasted_iota(jnp.int32, (128, R), 1)
    head_sum = ((hs_r < R) & (hs_c % T == hs_r % T)).astype(BF16)
    t_lane = past + lax.broadcasted_iota(jnp.int32, (1, 128), 1) % T
    n_slc = -(-(past + T) // SEL_BLOCK)
    n_slc_pad = -(-n_slc // 8) * 8
    new_blk = past // SEL_BLOCK
    kpos = lax.broadcasted_iota(jnp.int32, (1, past), 1)
    dist_s = (tq - kpos).astype(F32)
    tw = past - win_buf + lax.broadcasted_iota(jnp.int32, (1, win_buf), 1)
    dwin = tq - tw
    mask_w = (dwin >= 0) & (dwin < WINDOW) & (tw >= 0)
    dwin_f = dwin.astype(F32)

    def attend(s_past, mask_past, pv_past, new_scores, new_vals):
        s_past = jnp.where(mask_past, s_past, NEG)
        m = jnp.max(s_past, axis=-1, keepdims=True)
        for sj in new_scores:
            m = jnp.maximum(m, sj)
        e_past = jnp.where(mask_past, jnp.exp(s_past - m), 0.0)
        e_new = [jnp.where(sj > 0.5 * NEG, jnp.exp(sj - m), 0.0) for sj in new_scores]
        denom = jnp.sum(e_past, axis=-1, keepdims=True)
        for ej in e_new:
            denom = denom + ej
        out = pv_past((e_past / denom).astype(BF16))
        for ej, vj in zip(e_new, new_vals):
            out = out + _bf_round(ej / denom) * vj
        return out

    for g in range(G):
        rows = slice(g * R, (g + 1) * R)
        slope = jnp.zeros((R, 1), F32)
        for h in range(NSA_GROUP):
            slope = jnp.where(r_iota // T == h, 2.0 ** -(g * NSA_GROUP + h + 1), slope)
        q = q_ref[0, rows, :].astype(BF16)
        qf = q.astype(F32)

        k_cmp = compress_pages(kc_refs, posk_ref, phik_ref, g)
        v_cmp = compress_pages(vc_refs, posv_ref, phiv_ref, g)
        s = lax.dot_general(q, k_cmp, NT, preferred_element_type=F32) * scale - slope * dist_c
        p = _masked_softmax_rows(s, mask_c)
        o_cmp = jnp.dot(p.astype(BF16), v_cmp, preferred_element_type=F32)
        p_sum = sum(jnp.dot(head_sum, x, preferred_element_type=F32) for x in _split3(p))
        picked = _select_blocks(p_sum, t_lane, n_slc_pad, min(N_SEL, n_slc))[0:R]

        def new_keys(kn_ref, vn_ref, ok):
            scores, vals = [], []
            for j in range(T):
                kj = _bf_round(kn_ref[0, j * G + g:j * G + g + 1, :])
                sj = jnp.sum(qf * kj, axis=-1, keepdims=True) * scale - slope * (t_of_r - j).astype(F32)
                scores.append(jnp.where(ok & (t_of_r >= j), sj, NEG))
                vals.append(_bf_round(vn_ref[0, j * G + g:j * G + g + 1, :]))
            return scores, vals

        sel_past = jnp.dot(picked.astype(BF16), expand_ref[...], preferred_element_type=F32) > 0.5
        s_past = jnp.concatenate(
            [lax.dot_general(q, group_rows(ks_refs[p], g, PAGE_SIZE).astype(BF16), NT, preferred_element_type=F32)
             for p in range(n_pages)], axis=1) * scale - slope * dist_s

        def sel_pv(pb):
            out = jnp.zeros((R, HEAD_DIM), F32)
            for p in range(n_pages):
                out = out + jnp.dot(pb[:, p * PAGE_SIZE:(p + 1) * PAGE_SIZE],
                                    group_rows(vs_refs[p], g, PAGE_SIZE).astype(BF16), preferred_element_type=F32)
            return out

        o_sel = attend(s_past, sel_past, sel_pv, *new_keys(ksn_ref, vsn_ref, picked[:, new_blk:new_blk + 1] > 0.5))

        s_w = lax.dot_general(q, group_rows(wk_ref, g, win_buf).astype(BF16), NT,
                              preferred_element_type=F32) * scale - slope * dwin_f
        o_win = attend(s_w, mask_w,
                       lambda pb: jnp.dot(pb, group_rows(wv_ref, g, win_buf).astype(BF16), preferred_element_type=F32),
                       *new_keys(kwn_ref, vwn_ref, t_of_r >= 0))

        gts = gt_ref[0, rows, :]
        o_ref[0, rows, :] = gts[:, 0:1] * o_cmp + gts[:, 1:2] * o_sel + gts[:, 2:3] * o_win


def nsa_sample_attend(nq_ht, gates_ht, ks_new, vs_new, kw_new, vw_new, win_k, win_v, pos_k2, pos_v2, phi_k, phi_v,
                      page_table, ck_pool, cv_pool, sk_pool, sv_pool):
    B, RT, D = nq_ht.shape
    T = RT // NSA_HEADS
    G = NSA_KV_HEADS
    n_pages = page_table.shape[1]
    past = n_pages * PAGE_SIZE
    expand = (jnp.arange(past)[None, :] // SEL_BLOCK == jnp.arange(128)[:, None]).astype(BF16)
    assert (n_pages * PAGE_SIZE + T) // CMP_STRIDE == n_pages * PAGE_SIZE // CMP_STRIDE and T <= SEL_BLOCK
    per_b = lambda shape: pl.BlockSpec((1,) + shape, lambda b, pt: (b, 0, 0))
    const = lambda shape: pl.BlockSpec(shape, lambda b, pt: (0,) * len(shape))
    grid_spec = pltpu.PrefetchScalarGridSpec(
        num_scalar_prefetch=1, grid=(B,),
        in_specs=[per_b((RT, D)), per_b((RT, 3))] + [per_b((T * G, D))] * 4 + [per_b(win_k.shape[1:])] * 2
                 + [const((2, PAGE_SIZE, D))] * 2 + [const((HEAD_DIM, HEAD_DIM))] * 2 + [const(expand.shape)]
                 + [_page_spec(p, PAGE_SIZE * G) for p in range(n_pages)] * 4,
        out_specs=per_b((RT, D)))
    pools = [ck_pool] * n_pages + [cv_pool] * n_pages + [sk_pool] * n_pages + [sv_pool] * n_pages
    return pl.pallas_call(
        functools.partial(_nsa_sample_body, n_pages=n_pages, T=T),
        grid_spec=grid_spec,
        out_shape=jax.ShapeDtypeStruct((B, RT, D), F32),
        compiler_params=pltpu.CompilerParams(dimension_semantics=("arbitrary",), vmem_limit_bytes=VMEM_LIMIT),
        name="nsa_sample",
    )(page_table, nq_ht, gates_ht, ks_new, vs_new, kw_new, vw_new, win_k, win_v, pos_k2, pos_v2, phi_k, phi_v,
      expand, *pools)


def _tile_pos(w_pos):
    halves = w_pos.reshape(2, CMP_STRIDE, HEAD_DIM)
    return jnp.tile(halves, (1, PAGE_SIZE // CMP_STRIDE, 1))


def mixers_sample(fq, fk, fv, lf, nq, gates, kc, vc, ks, vs, kw, vw, win_k, win_v, page_table,
                  fox_k_pool, fox_v_pool, fox_lf_pool, cmp_k_pool, cmp_v_pool, slc_k_pool, slc_v_pool,
                  cmp_pos_k, cmp_phi_k, cmp_pos_v, cmp_phi_v):
    B, T = fq.shape[:2]
    n_phys = fox_k_pool.shape[0]
    past = page_table.shape[1] * PAGE_SIZE
    H = FOX_HEADS
    assert T * H <= HEAD_DIM
    lf_past = fox_lf_pool[page_table].reshape(B, past, H)
    c = jnp.cumsum(jnp.concatenate([lf_past, lf], axis=1), axis=1)
    c_new = c[:, past:].reshape(B, 1, T * H)
    pad_rows = lambda a: jnp.pad(a.reshape(B, T * H, HEAD_DIM), ((0, 0), (0, HEAD_DIM - T * H), (0, 0)))
    fox_o = fox_sample_attend(fq.reshape(B, T * H, HEAD_DIM), pad_rows(fk), pad_rows(fv),
                              c[:, :past].reshape(B, 1, past * H),
                              jnp.pad(c_new, ((0, 0), (0, 0), (0, HEAD_DIM - T * H))), c_new.reshape(B, T * H, 1),
                              page_table, fox_k_pool.reshape(n_phys, PAGE_SIZE * H, HEAD_DIM),
                              fox_v_pool.reshape(n_phys, PAGE_SIZE * H, HEAD_DIM)).reshape(B, T, FOX_W)
    ht = lambda a: jnp.swapaxes(a, 1, 2).reshape(B, NSA_HEADS * T, a.shape[-1])
    flat = lambda a: a.reshape(a.shape[0], a.shape[1] * NSA_KV_HEADS, HEAD_DIM)
    nsa_ht = nsa_sample_attend(ht(nq), ht(gates), flat(ks), flat(vs), flat(kw), flat(vw), flat(win_k), flat(win_v),
                               _tile_pos(cmp_pos_k), _tile_pos(cmp_pos_v), cmp_phi_k.astype(BF16), cmp_phi_v.astype(BF16),
                               page_table, flat(cmp_k_pool), flat(cmp_v_pool), flat(slc_k_pool), flat(slc_v_pool))
    nsa_o = jnp.swapaxes(nsa_ht.reshape(B, NSA_HEADS, T, HEAD_DIM), 1, 2).reshape(B, T, NSA_W)
    return fox_o, nsa_o


def merge_mixers(x, fox_o, nsa_o, g_fox_out, g_nsa_out, w_out_bf):
    B, T, D = x.shape
    n = B * T
    y = project([(fox_o.reshape(n, FOX_W), g_fox_out, w_out_bf[:FOX_W]),
                 (nsa_o.reshape(n, NSA_W), g_nsa_out, w_out_bf[FOX_W:])], res=x.reshape(n, D))
    return y.reshape(B, T, D)


def memory_kv(mem, g, w_mkv_bf):
    B, M, D = mem.shape
    kv = project([(mem.reshape(B * M, D), g, w_mkv_bf)])
    return (kv[:, :MEM_W].reshape(B, M, MEM_HEADS, HEAD_DIM), kv[:, MEM_W:].reshape(B, M, MEM_HEADS, HEAD_DIM))


def _mem_attend_body(q_ref, k_ref, v_ref, o_ref):
    H = MEM_HEADS
    q = q_ref[0].astype(BF16)
    s = lax.dot_general(q, k_ref[0].astype(BF16), NT, preferred_element_type=F32) * HEAD_DIM ** -0.5
    r_iota = lax.broadcasted_iota(jnp.int32, (s.shape[0], 1), 0)
    c_iota = lax.broadcasted_iota(jnp.int32, (1, s.shape[1]), 1)
    s = jnp.where((c_iota % H) == (r_iota % H), s, NEG)
    e = jnp.exp(s - jnp.max(s, axis=-1, keepdims=True))
    p = e / jnp.sum(e, axis=-1, keepdims=True)
    o_ref[0] = jnp.dot(p.astype(BF16), v_ref[0].astype(BF16), preferred_element_type=F32)


def memory_attend(x, g, w_mq_bf, w_mo_bf, mk, mv):
    B, T, D = x.shape
    M, H = mk.shape[1], MEM_HEADS
    x2 = x.reshape(B * T, D)
    q = project([(x2, g, w_mq_bf)]).reshape(B, T * H, HEAD_DIM)
    rows = min(MEM_ROW_TILE, T * H)
    kv_spec = pl.BlockSpec((1, M * H, HEAD_DIM), lambda b, i: (b, 0, 0))
    o = pl.pallas_call(
        _mem_attend_body,
        grid=(B, T * H // rows),
        in_specs=[pl.BlockSpec((1, rows, HEAD_DIM), lambda b, i: (b, i, 0)), kv_spec, kv_spec],
        out_specs=pl.BlockSpec((1, rows, HEAD_DIM), lambda b, i: (b, i, 0)),
        out_shape=jax.ShapeDtypeStruct((B, T * H, HEAD_DIM), F32),
        compiler_params=pltpu.CompilerParams(dimension_semantics=("arbitrary", "arbitrary"),
                                             vmem_limit_bytes=VMEM_LIMIT),
        name="mem_attend",
    )(q, mk.reshape(B, M * H, HEAD_DIM), mv.reshape(B, M * H, HEAD_DIM))
    return project([(o.reshape(B * T, MEM_W), None, w_mo_bf)], res=x2).reshape(B, T, D)


def _top16_rows(s):
    iota = lax.broadcasted_iota(jnp.int32, s.shape, 0)
    vals = []
    for _ in range(PEER_TOPK):
        m = jnp.max(s, axis=0, keepdims=True)
        idx = jnp.min(jnp.where(s == m, iota, PEER_KEYS), axis=0, keepdims=True)
        s = jnp.where(iota == idx, NEG_INF, s)
        vals.append(m)
    return s == NEG_INF, jnp.concatenate(vals, axis=0)


def _peer_scores_body(x_ref, g_ref, wq_ref, k1_ref, k2_ref, xn_ref, a_ref, b_ref, tau_ref):
    h = pl.program_id(1)

    @pl.when(h == 0)
    def _():
        xf = x_ref[...]
        y = xf * lax.rsqrt(jnp.mean(xf * xf, axis=-1, keepdims=True) + EPS) * g_ref[...]
        xn_ref[...] = y.astype(BF16)

    qT = lax.dot_general(wq_ref[...], xn_ref[...], (((1,), (1,)), ((), ())), preferred_element_type=F32)
    half = qT.shape[0] // 2
    s1 = jnp.dot(k1_ref[...], qT[:half].astype(BF16), preferred_element_type=F32)
    s2 = jnp.dot(k2_ref[...], qT[half:].astype(BF16), preferred_element_type=F32)
    mem1, v1 = _top16_rows(s1)
    mem2, v2 = _top16_rows(s2)
    pieces = [v1[0:1] + v2]
    for r in range(1, 8):
        pieces.append(v1[r:r + 1] + v2[0:8])
    pieces.append(v1[8:16] + v2[0:1])
    c = jnp.concatenate(pieces, axis=0)
    rank = jnp.zeros(c.shape, F32)
    for j in range(c.shape[0]):
        rank = rank + jnp.where(c[j:j + 1] > c, 1.0, 0.0)
    c16 = jnp.min(jnp.where(rank <= PEER_TOPK - 1, c, jnp.inf), axis=0, keepdims=True)
    c17 = jnp.min(jnp.where(rank <= PEER_TOPK, c, jnp.inf), axis=0, keepdims=True)
    z = jnp.sum(jnp.where(c >= c16, jnp.exp(c - c[0:1]), 0.0), axis=0, keepdims=True)
    shift = v1[0:1] + jnp.log(z)
    log2e = math.log2(math.e)
    a_ref[0] = jnp.where(mem1, (s1 - shift) * log2e, NEG_INF)
    b_ref[0] = jnp.where(mem2, (s2 - v2[0:1]) * log2e, NEG_INF)
    tau_ref[0] = (0.5 * (c16 + c17) - shift - v2[0:1]) * log2e


def peer_scores(x2, g, wqT_bf, k1_bf, k2_bf, tt):
    n, d = x2.shape
    H = PEER_HEADS
    qd = wqT_bf.shape[0] // H
    return pl.pallas_call(
        _peer_scores_body,
        grid=(n // tt, H),
        in_specs=[pl.BlockSpec((tt, d), lambda i, h: (i, 0)),
                  pl.BlockSpec((1, d), lambda i, h: (0, 0)),
                  pl.BlockSpec((qd, d), lambda i, h: (h, 0)),
                  pl.BlockSpec((PEER_KEYS, qd // 2), lambda i, h: (0, 0)),
                  pl.BlockSpec((PEER_KEYS, qd // 2), lambda i, h: (0, 0))],
        out_specs=[pl.BlockSpec((tt, d), lambda i, h: (i, 0))]
                  + [pl.BlockSpec((1, PEER_KEYS, tt), lambda i, h: (h, 0, i))] * 2
                  + [pl.BlockSpec((1, 1, tt), lambda i, h: (h, 0, i))],
        out_shape=[jax.ShapeDtypeStruct((n, d), BF16)]
                  + [jax.ShapeDtypeStruct((H, PEER_KEYS, n), F32)] * 2
                  + [jax.ShapeDtypeStruct((H, 1, n), F32)],
        compiler_params=pltpu.CompilerParams(dimension_semantics=("arbitrary", "arbitrary"),
                                             vmem_limit_bytes=VMEM_LIMIT),
        name="peer_scores",
    )(x2, g.reshape(1, d), wqT_bf, k1_bf, k2_bf)


def _gelu_exact(x):
    return 0.5 * x * (1.0 + lax.erf(x * (2.0 ** -0.5)))


def _peer_dense_body(x_ref, gf_ref, xn_ref, u_ref, vt_ref, a_ref, b_ref, tau_ref, o_ref, acc_ref, h_ref, p_ref,
                     *, te, final_norm):
    j = pl.program_id(1)

    @pl.when(j == 0)
    def _():
        acc_ref[...] = jnp.zeros(acc_ref.shape, F32)

    na = te // PEER_KEYS
    lane_tile = PEER_KEYS
    for k in range(PEER_SUBTILES):
        sub = slice(k * te // PEER_SUBTILES, (k + 1) * te // PEER_SUBTILES)
        h_ref[sub, :] = lax.dot_general(u_ref[sub, :], xn_ref[...], NT, preferred_element_type=F32)
    for k in range(PEER_SUBTILES):
        sub = slice(k * te // PEER_SUBTILES, (k + 1) * te // PEER_SUBTILES)
        for al in range(k * na // PEER_SUBTILES, (k + 1) * na // PEER_SUBTILES):
            a = j * na + al
            rows = slice(al * PEER_KEYS, (al + 1) * PEER_KEYS)
            a_rows = [a_ref[h, pl.ds(a, 1), :] for h in range(PEER_HEADS)]
            for c in range(h_ref.shape[1] // lane_tile):
                lanes = slice(c * lane_tile, (c + 1) * lane_tile)
                w = jnp.zeros((PEER_KEYS, lane_tile), F32)
                for h in range(PEER_HEADS):
                    t = a_rows[h][:, lanes] + b_ref[h, :, lanes]
                    w = w + jnp.where(t >= tau_ref[h, :, lanes], jnp.exp2(t), 0.0)
                p_ref[rows, lanes] = (w * _gelu_exact(h_ref[rows, lanes])).astype(BF16)
        acc_ref[...] += jnp.dot(vt_ref[:, sub], p_ref[sub, :], preferred_element_type=F32)

    @pl.when(j == pl.num_programs(1) - 1)
    def _():
        y = x_ref[...] + acc_ref[...].T
        if final_norm:
            y = y * lax.rsqrt(jnp.mean(y * y, axis=-1, keepdims=True) + EPS) * gf_ref[...]
        o_ref[...] = y


def peer_dense(x2, g_final, xn_bf, u_bf, vT_bf, a, b, tau, tt, te, final_norm):
    n, d = xn_bf.shape
    ne = u_bf.shape[0]
    H = PEER_HEADS
    return pl.pallas_call(
        functools.partial(_peer_dense_body, te=te, final_norm=final_norm),
        grid=(n // tt, ne // te),
        in_specs=[pl.BlockSpec((tt, d), lambda i, j: (i, 0), pipeline_mode=pl.Buffered(1)),
                  pl.BlockSpec((1, d), lambda i, j: (0, 0)),
                  pl.BlockSpec((tt, d), lambda i, j: (i, 0)),
                  pl.BlockSpec((te, d), lambda i, j: (j, 0)),
                  pl.BlockSpec((d, te), lambda i, j: (0, j))]
                 + [pl.BlockSpec((H, PEER_KEYS, tt), lambda i, j: (0, 0, i), pipeline_mode=pl.Buffered(1))] * 2
                 + [pl.BlockSpec((H, 1, tt), lambda i, j: (0, 0, i))],
        out_specs=pl.BlockSpec((tt, d), lambda i, j: (i, 0)),
        out_shape=jax.ShapeDtypeStruct((n, d), F32),
        scratch_shapes=[pltpu.VMEM((d, tt), F32), pltpu.VMEM((te, tt), F32), pltpu.VMEM((te, tt), BF16)],
        compiler_params=pltpu.CompilerParams(dimension_semantics=("arbitrary", "arbitrary"),
                                             vmem_limit_bytes=VMEM_LIMIT),
        name="peer_dense",
    )(x2, g_final.reshape(1, d), xn_bf, u_bf, vT_bf, a, b, tau)


def peer_ffn(x, g, wqT_bf, k1_bf, k2_bf, u_bf, vT_bf, g_final, final_norm):
    B, T, D = x.shape
    x2 = x.reshape(B * T, D)
    tt = min(PEER_TOKEN_TILE, B * T)
    assert (B * T) % tt == 0
    xn_bf, a, b, tau = peer_scores(x2, g, wqT_bf, k1_bf, k2_bf, tt)
    out = peer_dense(x2, g_final, xn_bf, u_bf, vT_bf, a, b, tau, tt, PEER_EXPERT_TILE, final_norm)
    return out.reshape(B, T, D)


def kernel(x_prompt, x_sample, mem_prompt, cache_fox_k, cache_fox_v, cache_fox_logf,
           cache_cmp_k, cache_cmp_v, cache_slc_k, cache_slc_v, state_win_k, state_win_v,
           cache_mem_k, cache_mem_v, page_table,
           g_mix, w_in, b_fox_f, b_nsa_gate, cmp_pos_k, cmp_phi_k, cmp_pos_v, cmp_phi_v,
           g_fox_out, g_nsa_out, w_out, g_mem_q, g_mem_kv, w_mq, w_mk, w_mv, w_mo,
           g_peer, w_pq, peer_subkey_1, peer_subkey_2, peer_u, peer_v, g_final):
    depth = g_mix.shape[0]
    win_buf = min(WINDOW, PAST_LEN)
    names_p = ('fox_k', 'fox_v', 'fox_logf', 'cmp_k', 'cmp_v', 'slc_k', 'slc_v', 'win_k', 'win_v', 'mem_k', 'mem_v')
    names_s = ('fox_k', 'fox_v', 'fox_logf', 'cmp_k', 'cmp_v', 'slc_k', 'slc_v', 'win_k', 'win_v')
    sp = {n: [] for n in names_p}
    ss = {n: [] for n in names_s}
    xp, xs = x_prompt, x_sample
    for l in range(depth):
        w_in_bf = regroup_w_in(w_in[l])
        w_out_bf, w_mq_bf, w_mo_bf = w_out[l].astype(BF16), w_mq[l].astype(BF16), w_mo[l].astype(BF16)
        w_mkv_bf = jnp.concatenate([w_mk[l], w_mv[l]], axis=1).astype(BF16)
        peer_w = (w_pq[l].T.astype(BF16), peer_subkey_1[l].astype(BF16), peer_subkey_2[l].astype(BF16),
                  peer_u[l].astype(BF16), peer_v[l].T.astype(BF16))

        fq, fk, fv, lf, nq, kc, vc, ks, vs, kw, vw, gt = project_mixers(xp, g_mix[l], w_in_bf, b_fox_f[l], b_nsa_gate[l])
        fox_o = fox_prompt(fq, fk, fv, lf)
        nsa_o = nsa_prompt(nq, gt, kc, vc, ks, vs, kw, vw, cmp_pos_k[l], cmp_phi_k[l], cmp_pos_v[l], cmp_phi_v[l])
        xp = merge_mixers(xp, fox_o, nsa_o, g_fox_out[l], g_nsa_out[l], w_out_bf)
        mk, mv = memory_kv(mem_prompt, g_mem_kv[l], w_mkv_bf)
        xp = memory_attend(xp, g_mem_q[l], w_mq_bf, w_mo_bf, mk, mv)
        last = l == depth - 1
        xp = peer_ffn(xp, g_peer[l], *peer_w, g_final, last)
        for name, val in zip(names_p, (fk, fv, lf, kc, vc, ks, vs, last_rows(kw, win_buf), last_rows(vw, win_buf), mk, mv)):
            sp[name].append(val)

        fq, fk, fv, lf, nq, kc, vc, ks, vs, kw, vw, gt = project_mixers(xs, g_mix[l], w_in_bf, b_fox_f[l], b_nsa_gate[l])
        kw_all = jnp.concatenate([state_win_k[l].astype(kw.dtype), kw], axis=1)
        vw_all = jnp.concatenate([state_win_v[l].astype(vw.dtype), vw], axis=1)
        fox_o, nsa_o = mixers_sample(fq, fk, fv, lf, nq, gt, kc, vc, ks, vs, kw, vw,
                                     state_win_k[l], state_win_v[l], page_table,
                                     cache_fox_k[l], cache_fox_v[l], cache_fox_logf[l],
                                     cache_cmp_k[l], cache_cmp_v[l], cache_slc_k[l], cache_slc_v[l],
                                     cmp_pos_k[l], cmp_phi_k[l], cmp_pos_v[l], cmp_phi_v[l])
        xs = merge_mixers(xs, fox_o, nsa_o, g_fox_out[l], g_nsa_out[l], w_out_bf)
        xs = memory_attend(xs, g_mem_q[l], w_mq_bf, w_mo_bf, cache_mem_k[l], cache_mem_v[l])
        xs = peer_ffn(xs, g_peer[l], *peer_w, g_final, last)
        for name, val in zip(names_s, (fk, fv, lf, kc, vc, ks, vs, kw_all[:, -win_buf:], vw_all[:, -win_buf:])):
            ss[name].append(val)

    y_prompt, y_sample = (xp, xs) if depth else (rmsnorm_pallas(xp, g_final), rmsnorm_pallas(xs, g_final))
    sp = {n: jnp.stack(v) for n, v in sp.items()}
    ss = {n: jnp.stack(v) for n, v in ss.items()}
    return (y_prompt, y_sample,
            sp['fox_k'], sp['fox_v'], sp['fox_logf'], sp['cmp_k'], sp['cmp_v'], sp['slc_k'], sp['slc_v'],
            sp['win_k'], sp['win_v'], sp['mem_k'], sp['mem_v'],
            ss['fox_k'], ss['fox_v'], ss['fox_logf'], ss['cmp_k'], ss['cmp_v'], ss['slc_k'], ss['slc_v'],
            ss['win_k'], ss['win_v'])
```

```python
import functools
import math
import jax
import jax.numpy as jnp
from jax import lax
import numpy as np
from jax.experimental import pallas as pl
from jax.experimental.pallas import tpu as pltpu

PAST_LEN = 2048
PAGE_SIZE = 128

HEAD_DIM = 128
FOX_HEADS = 8
NSA_HEADS = 8
NSA_KV_HEADS = 2
NSA_GROUP = NSA_HEADS // NSA_KV_HEADS
CMP_STRIDE = 16
CMP_BLOCK = 2 * CMP_STRIDE
SEL_BLOCK = 64
N_SEL = 16
WINDOW = 512
MEM_HEADS = 4
PEER_HEADS = 8
PEER_KEYS = 128
PEER_TOPK = 16
Q_BLOCK = 128
FORCE_BONUS = 1.0e4
NEG = -1.0e30
EPS = 1e-6

FOX_W = FOX_HEADS * HEAD_DIM
NSA_W = NSA_HEADS * HEAD_DIM
KV_W = NSA_KV_HEADS * HEAD_DIM
MEM_W = MEM_HEADS * HEAD_DIM
IN_SIZES = (FOX_W, FOX_W, FOX_W, FOX_HEADS, NSA_W, KV_W, KV_W, KV_W, KV_W, KV_W, KV_W, 3 * NSA_HEADS)
P_IN = sum(IN_SIZES)

F32 = jnp.float32
BF16 = jnp.bfloat16
NEG_INF = float("-inf")
VMEM_LIMIT = 56 * 1024 * 1024
PEER_TOKEN_TILE = 512
PEER_EXPERT_TILE = 1024
PEER_SUBTILES = 2
SLC_CHUNK = 512
FOX_CHUNK = 512
PROJ_ROW_TILE = 512
PROJ_COL_TILE = 512
PROJ_IN_COL_TILE = 640
MEM_ROW_TILE = 512
NT = (((1,), (1,)), ((), ()))


def _rmsnorm_body(x_ref, g_ref, o_ref):
    xf = x_ref[...]
    y = xf * lax.rsqrt(jnp.mean(xf * xf, axis=-1, keepdims=True) + EPS)
    o_ref[...] = y * g_ref[...]


def rmsnorm_pallas(x, g, rows=256):
    shp = x.shape
    d = shp[-1]
    x2 = x.reshape(-1, d)
    n = x2.shape[0]
    rows = min(rows, n)
    assert n % rows == 0
    out = pl.pallas_call(
        _rmsnorm_body,
        grid=(n // rows,),
        in_specs=[pl.BlockSpec((rows, d), lambda i: (i, 0)),
                  pl.BlockSpec((1, d), lambda i: (0, 0))],
        out_specs=pl.BlockSpec((rows, d), lambda i: (i, 0)),
        out_shape=jax.ShapeDtypeStruct((n, d), F32),
        name="final_rmsnorm",
    )(x2, g.reshape(1, d))
    return out.reshape(shp)


def last_rows(a, n):
    T = a.shape[1]
    if T < n:
        a = jnp.pad(a, ((0, 0), (n - T, 0)) + ((0, 0),) * (a.ndim - 2))
    return a[:, -n:]


def _proj_body(*refs, n_in, normed, has_res, bf16_copy):
    n_refs_in = n_in * (3 if normed else 2) + (1 if has_res else 0)
    o_ref = refs[n_refs_in]
    n_out = 2 if bf16_copy else 1
    xn_refs = refs[n_refs_in + n_out:]
    per = 3 if normed else 2

    @pl.when(pl.program_id(1) == 0)
    def _():
        for k in range(n_in):
            x = refs[k * per][...]
            if normed:
                x = x * lax.rsqrt(jnp.mean(x * x, axis=-1, keepdims=True) + EPS) * refs[k * per + 1][...]
            xn_refs[k][...] = x.astype(BF16)

    acc = jnp.dot(xn_refs[0][...], refs[per - 1][...], preferred_element_type=F32)
    for k in range(1, n_in):
        acc = acc + jnp.dot(xn_refs[k][...], refs[k * per + per - 1][...], preferred_element_type=F32)
    if has_res:
        acc = acc + refs[n_in * per][...]
    o_ref[...] = acc
    if bf16_copy:
        refs[n_refs_in + 1][...] = acc.astype(BF16)


def project(inputs, res=None, tn=PROJ_COL_TILE, bf16_copy=False):
    n = inputs[0][0].shape[0]
    N = inputs[0][2].shape[1]
    normed = inputs[0][1] is not None
    tm = min(PROJ_ROW_TILE, n)
    tn = min(tn, N)
    assert n % tm == 0 and N % tn == 0
    args, in_specs, scratch = [], [], []
    for x, g, w in inputs:
        d = x.shape[1]
        args.append(x)
        in_specs.append(pl.BlockSpec((tm, d), lambda i, j: (i, 0)))
        if normed:
            args.append(g.reshape(1, d))
            in_specs.append(pl.BlockSpec((1, d), lambda i, j: (0, 0)))
        args.append(w)
        in_specs.append(pl.BlockSpec((d, tn), lambda i, j: (0, j)))
        scratch.append(pltpu.VMEM((tm, d), BF16))
    if res is not None:
        args.append(res)
        in_specs.append(pl.BlockSpec((tm, tn), lambda i, j: (i, j)))
    out_spec = pl.BlockSpec((tm, tn), lambda i, j: (i, j))
    out_shape = jax.ShapeDtypeStruct((n, N), F32)
    return pl.pallas_call(
        functools.partial(_proj_body, n_in=len(inputs), normed=normed, has_res=res is not None, bf16_copy=bf16_copy),
        grid=(n // tm, N // tn),
        in_specs=in_specs,
        out_specs=[out_spec, out_spec] if bf16_copy else out_spec,
        out_shape=[out_shape, jax.ShapeDtypeStruct((n, N), BF16)] if bf16_copy else out_shape,
        scratch_shapes=scratch,
        compiler_params=pltpu.CompilerParams(dimension_semantics=("arbitrary", "arbitrary"),
                                             vmem_limit_bytes=VMEM_LIMIT),
        name="project",
    )(*args)


_IN_OFFS = [int(o) for o in np.cumsum((0,) + IN_SIZES)]
_IN_ORDER = (0, 1, 2, 4, 5, 6, 7, 8, 9, 10, 3, 11)
P_IN_PAD = -(-P_IN // PROJ_IN_COL_TILE) * PROJ_IN_COL_TILE


def regroup_w_in(w_in):
    cols = [w_in[:, _IN_OFFS[k]:_IN_OFFS[k + 1]] for k in _IN_ORDER]
    cols.append(jnp.zeros((w_in.shape[0], P_IN_PAD - P_IN), w_in.dtype))
    return jnp.concatenate(cols, axis=1).astype(BF16)


def _regrouped_cols():
    cols, off = {}, 0
    for k in _IN_ORDER:
        cols[k] = off
        off += IN_SIZES[k]
    return cols


_IN_COL = _regrouped_cols()


def project_mixers(x, g_mix, w_in_bf, b_fox_f, b_nsa_gate, bf16_copy=False):
    B, T, D = x.shape
    out = project([(x.reshape(B * T, D), g_mix, w_in_bf)], tn=PROJ_IN_COL_TILE, bf16_copy=bf16_copy)
    y, y_bf = out if bf16_copy else (out, None)
    parts = {k: y[:, _IN_COL[k]:_IN_COL[k] + IN_SIZES[k]] for k in _IN_ORDER}
    hd = lambda a: a.reshape(B, T, -1, HEAD_DIM)
    logf = jax.nn.log_sigmoid(parts[3] + b_fox_f.astype(F32)).reshape(B, T, FOX_HEADS)
    gates = jax.nn.sigmoid(parts[11] + b_nsa_gate.astype(F32)).reshape(B, T, NSA_HEADS, 3)
    packed = (y.reshape(B, T, P_IN_PAD), y_bf.reshape(B, T, P_IN_PAD)) if bf16_copy else None
    return (hd(parts[0]), hd(parts[1]), hd(parts[2]), logf, hd(parts[4]), hd(parts[5]), hd(parts[6]),
            hd(parts[7]), hd(parts[8]), hd(parts[9]), hd(parts[10]), gates), packed


def _col_window(rows, width, group, row_block):
    assert _IN_COL[group] % width == 0 and IN_SIZES[group] == width
    col = _IN_COL[group] // width
    if row_block:
        return pl.BlockSpec((1, rows, width), lambda b, i: (b, i, col))
    return pl.BlockSpec((1, rows, width), lambda b, i: (b, 0, col))


def _fox_prompt_body(q_ref, k_ref, v_ref, cq_ref, ck_ref, o_ref, m_ref, l_ref, acc_ref):
    i = pl.program_id(1)
    H = FOX_HEADS
    scale = HEAD_DIM ** -0.5
    t_row = i * Q_BLOCK + lax.broadcasted_iota(jnp.int32, (1, Q_BLOCK, 1), 1)
    n_chunks = (i * Q_BLOCK + Q_BLOCK + FOX_CHUNK - 1) // FOX_CHUNK
    head_cols = [slice(h * HEAD_DIM, (h + 1) * HEAD_DIM) for h in range(H)]
    qs = [q_ref[0, :, cols].astype(BF16) for cols in head_cols]
    cq_all = cq_ref[0]
    cq = jnp.stack([cq_all[:, h:h + 1] for h in range(H)])
    m_ref[...] = jnp.full(m_ref.shape, NEG, F32)
    l_ref[...] = jnp.zeros(l_ref.shape, F32)
    acc_ref[...] = jnp.zeros(acc_ref.shape, F32)

    def chunk(c, carry):
        k0 = pl.multiple_of(c * FOX_CHUNK, FOX_CHUNK)
        kpos = k0 + lax.broadcasted_iota(jnp.int32, (1, 1, FOX_CHUNK), 2)
        mask = kpos <= t_row
        s = jnp.stack([lax.dot_general(qs[h], k_ref[0, pl.ds(k0, FOX_CHUNK), head_cols[h]], NT,
                                       preferred_element_type=F32) for h in range(H)]) * scale
        s = jnp.where(mask, s + cq - ck_ref[0, :, :, pl.ds(k0, FOX_CHUNK)], NEG)
        m_old = m_ref[...]
        m_new = jnp.maximum(m_old, jnp.max(s, axis=-1, keepdims=True))
        alpha = jnp.exp(m_old - m_new)
        p = jnp.where(mask, jnp.exp(s - m_new), 0.0)
        m_ref[...] = m_new
        l_ref[...] = alpha * l_ref[...] + jnp.sum(p, axis=-1, keepdims=True)
        pb = p.astype(BF16)
        pv = jnp.stack([jnp.dot(pb[h], v_ref[0, pl.ds(k0, FOX_CHUNK), head_cols[h]], preferred_element_type=F32)
                        for h in range(H)])
        acc_ref[...] = alpha * acc_ref[...] + pv
        return carry

    lax.fori_loop(0, n_chunks, chunk, 0)
    o = acc_ref[...] / l_ref[...]
    for h in range(H):
        o_ref[0, :, head_cols[h]] = o[h]


def fox_prompt(packed, logf):
    y_bf = packed[1]
    B, S = logf.shape[:2]
    assert S % FOX_CHUNK == 0
    H = FOX_HEADS
    c = jnp.cumsum(logf, axis=1)
    block = lambda width: pl.BlockSpec((1, Q_BLOCK, width), lambda b, i: (b, i, 0))
    return pl.pallas_call(
        _fox_prompt_body,
        grid=(B, S // Q_BLOCK),
        in_specs=[_col_window(Q_BLOCK, FOX_W, 0, True), _col_window(S, FOX_W, 1, False),
                  _col_window(S, FOX_W, 2, False), block(H),
                  pl.BlockSpec((1, H, 1, S), lambda b, i: (b, 0, 0, 0))],
        out_specs=block(FOX_W),
        out_shape=jax.ShapeDtypeStruct((B, S, FOX_W), F32),
        scratch_shapes=[pltpu.VMEM((H, Q_BLOCK, 1), F32), pltpu.VMEM((H, Q_BLOCK, 1), F32),
                        pltpu.VMEM((H, Q_BLOCK, HEAD_DIM), F32)],
        compiler_params=pltpu.CompilerParams(dimension_semantics=("arbitrary", "arbitrary"),
                                             vmem_limit_bytes=VMEM_LIMIT),
        name="fox_prompt",
    )(y_bf, y_bf, y_bf, c, jnp.swapaxes(c, 1, 2).reshape(B, H, 1, S))


def compress(k, w_pos, w_phi):
    B, L, G, D = k.shape
    n_chunk = L // CMP_STRIDE
    c = k[:, :n_chunk * CMP_STRIDE].reshape(B, n_chunk, CMP_STRIDE, G, D)
    blocks = jnp.concatenate([c[:, :-1], c[:, 1:]], axis=2)
    pooled = jnp.einsum('bnlgd,ld->bngd', blocks, w_pos)
    return pooled @ w_phi


def _masked_softmax_rows(s, mask):
    s = jnp.where(mask, s, NEG)
    e = jnp.exp(s - jnp.max(s, axis=-1, keepdims=True))
    p = e / jnp.sum(e, axis=-1, keepdims=True)
    return jnp.where(mask, p, 0.0)


def _split3(x):
    hi = x.astype(BF16)
    r = x - hi.astype(F32)
    mid = r.astype(BF16)
    lo = (r - mid.astype(F32)).astype(BF16)
    return hi, mid, lo


def _select_blocks(p_sum, t_lane, n_slc_pad, k_sel):
    T = p_sum.shape[0]
    s_iota = lax.broadcasted_iota(jnp.int32, (n_slc_pad, 1), 0)
    n_iota = lax.broadcasted_iota(jnp.int32, (1, p_sum.shape[1]), 1)
    ratio = SEL_BLOCK // CMP_STRIDE
    selT = ((n_iota < ratio * s_iota + ratio) & (n_iota > ratio * s_iota - CMP_BLOCK // CMP_STRIDE)).astype(BF16)
    impT = sum(lax.dot_general(selT, x, NT, preferred_element_type=F32) for x in _split3(p_sum))
    curT = t_lane // SEL_BLOCK
    validT = s_iota <= curT
    forcedT = (s_iota == 0) | (s_iota == curT) | (s_iota == curT - 1)
    scoreT = jnp.where(validT, impT + jnp.where(forcedT, FORCE_BONUS, 0.0), NEG)
    rank = jnp.zeros(scoreT.shape, F32)
    for sp in range(n_slc_pad):
        row = scoreT[sp:sp + 1]
        beats = (row > scoreT) | ((row == scoreT) & (s_iota > sp))
        rank = rank + jnp.where(beats, 1.0, 0.0)
    pickedT = jnp.where(validT & (rank < k_sel), 1.0, 0.0)
    pad = jnp.zeros((128 - n_slc_pad, T), F32)
    return jnp.concatenate([pickedT, pad], axis=0).T


def _nsa_prompt_body(q_ref, gt_ref, kcmp_ref, vcmp_ref, ks_ref, vs_ref, kw_ref, vw_ref, expand_ref, o_ref,
                     m_ref, l_ref, acc_ref, *, n_slc):
    i = pl.program_id(1)
    scale = HEAD_DIM ** -0.5
    R = NSA_GROUP * Q_BLOCK
    r_iota = lax.broadcasted_iota(jnp.int32, (R, 1), 0)
    t_row = i * Q_BLOCK + r_iota % Q_BLOCK
    t_lane = i * Q_BLOCK + lax.broadcasted_iota(jnp.int32, (1, Q_BLOCK), 1)
    gts = gt_ref[0]
    n_iota = lax.broadcasted_iota(jnp.int32, (1, 128), 1)
    t_cmp = n_iota * CMP_STRIDE + (CMP_BLOCK - 1)
    dist_c = (t_row - t_cmp).astype(F32)
    mask_c = t_cmp <= t_row
    n_chunks = (i * Q_BLOCK + Q_BLOCK + SLC_CHUNK - 1) // SLC_CHUNK
    w0 = pl.multiple_of(jnp.maximum(i - WINDOW // Q_BLOCK, 0) * Q_BLOCK, Q_BLOCK)
    tw = w0 + lax.broadcasted_iota(jnp.int32, (1, WINDOW + Q_BLOCK), 1)
    dwin = t_row - tw
    mask_w = (dwin >= 0) & (dwin < WINDOW)
    dwin_f = dwin.astype(F32)

    for g in range(NSA_KV_HEADS):
        cols = slice(g * HEAD_DIM, (g + 1) * HEAD_DIM)
        heads = [g * NSA_GROUP + h for h in range(NSA_GROUP)]
        slope = jnp.zeros((R, 1), F32)
        for h, hd in enumerate(heads):
            slope = jnp.where(r_iota // Q_BLOCK == h, 2.0 ** -(hd + 1), slope)
        q = jnp.concatenate([q_ref[0, :, hd * HEAD_DIM:(hd + 1) * HEAD_DIM] for hd in heads], axis=0).astype(BF16)

        s = lax.dot_general(q, kcmp_ref[0, :, cols], NT, preferred_element_type=F32) * scale - slope * dist_c
        p = _masked_softmax_rows(s, mask_c)
        o_cmp = jnp.dot(p.astype(BF16), vcmp_ref[0, :, cols], preferred_element_type=F32)
        p_sum = functools.reduce(jnp.add, [p[h * Q_BLOCK:(h + 1) * Q_BLOCK] for h in range(NSA_GROUP)])
        picked = _select_blocks(p_sum, t_lane, n_slc, min(N_SEL, n_slc)).astype(BF16)
        picked_rows = jnp.concatenate([picked] * NSA_GROUP, axis=0)

        m_ref[...] = jnp.full(m_ref.shape, NEG, F32)
        l_ref[...] = jnp.zeros(l_ref.shape, F32)
        acc_ref[...] = jnp.zeros(acc_ref.shape, F32)

        def chunk(c, carry):
            k0 = pl.multiple_of(c * SLC_CHUNK, SLC_CHUNK)
            kpos = k0 + lax.broadcasted_iota(jnp.int32, (1, SLC_CHUNK), 1)
            sel = jnp.dot(picked_rows, expand_ref[:, pl.ds(k0, SLC_CHUNK)], preferred_element_type=F32)
            mask = (sel > 0.5) & (kpos <= t_row)
            s = lax.dot_general(q, ks_ref[0, pl.ds(k0, SLC_CHUNK), cols], NT, preferred_element_type=F32) * scale
            s = jnp.where(mask, s - slope * (t_row - kpos).astype(F32), NEG)
            m_old = m_ref[...]
            m_new = jnp.maximum(m_old, jnp.max(s, axis=-1, keepdims=True))
            alpha = jnp.exp(m_old - m_new)
            p = jnp.where(mask, jnp.exp(s - m_new), 0.0)
            m_ref[...] = m_new
            l_ref[...] = alpha * l_ref[...] + jnp.sum(p, axis=-1, keepdims=True)
            acc_ref[...] = alpha * acc_ref[...] + jnp.dot(p.astype(BF16), vs_ref[0, pl.ds(k0, SLC_CHUNK), cols],
                                                          preferred_element_type=F32)
            return carry

        lax.fori_loop(0, n_chunks, chunk, 0)
        o_sel = acc_ref[...] / l_ref[...]

        s = lax.dot_general(q, kw_ref[0, pl.ds(w0, WINDOW + Q_BLOCK), cols], NT,
                            preferred_element_type=F32) * scale - slope * dwin_f
        p = _masked_softmax_rows(s, mask_w)
        o_win = jnp.dot(p.astype(BF16), vw_ref[0, pl.ds(w0, WINDOW + Q_BLOCK), cols], preferred_element_type=F32)

        gate = lambda j: jnp.concatenate([gts[:, 3 * hd + j:3 * hd + j + 1] for hd in heads], axis=0)
        o = gate(0) * o_cmp + gate(1) * o_sel + gate(2) * o_win
        for h, hd in enumerate(heads):
            o_ref[0, :, hd * HEAD_DIM:(hd + 1) * HEAD_DIM] = o[h * Q_BLOCK:(h + 1) * Q_BLOCK]


def nsa_prompt_attend(y_bf, gates, k_cmp, v_cmp):
    B, S = gates.shape[:2]
    W = NSA_W
    assert S % SLC_CHUNK == 0 and S >= WINDOW + Q_BLOCK and S // SEL_BLOCK <= 128
    n_slc = S // SEL_BLOCK
    expand = (jnp.arange(S)[None, :] // SEL_BLOCK == jnp.arange(128)[:, None]).astype(BF16)
    rows = NSA_GROUP * Q_BLOCK
    return pl.pallas_call(
        functools.partial(_nsa_prompt_body, n_slc=n_slc),
        grid=(B, S // Q_BLOCK),
        in_specs=[_col_window(Q_BLOCK, W, 4, True),
                  pl.BlockSpec((1, Q_BLOCK, gates.shape[-1]), lambda b, i: (b, i, 0)),
                  pl.BlockSpec((1, 128, KV_W), lambda b, i: (b, 0, 0)),
                  pl.BlockSpec((1, 128, KV_W), lambda b, i: (b, 0, 0))]
                 + [_col_window(S, KV_W, group, False) for group in (7, 8, 9, 10)]
                 + [pl.BlockSpec((128, S), lambda b, i: (0, 0))],
        out_specs=pl.BlockSpec((1, Q_BLOCK, W), lambda b, i: (b, i, 0)),
        out_shape=jax.ShapeDtypeStruct((B, S, W), F32),
        scratch_shapes=[pltpu.VMEM((rows, 1), F32), pltpu.VMEM((rows, 1), F32), pltpu.VMEM((rows, HEAD_DIM), F32)],
        compiler_params=pltpu.CompilerParams(dimension_semantics=("arbitrary", "arbitrary"),
                                             vmem_limit_bytes=VMEM_LIMIT),
        name="nsa_prompt",
    )(y_bf, gates, k_cmp, v_cmp, y_bf, y_bf, y_bf, y_bf, expand)


def _pad_cmp(c):
    B, n = c.shape[:2]
    return jnp.pad(c.reshape(B, n, KV_W), ((0, 0), (0, 128 - n), (0, 0))).astype(BF16)


def nsa_prompt(packed, gates, kc, vc, cmp_pos_k, cmp_phi_k, cmp_pos_v, cmp_phi_v):
    B, S = gates.shape[:2]
    k_cmp = _pad_cmp(compress(kc, cmp_pos_k, cmp_phi_k))
    v_cmp = _pad_cmp(compress(vc, cmp_pos_v, cmp_phi_v))
    return nsa_prompt_attend(packed[1], gates.reshape(B, S, 3 * NSA_HEADS), k_cmp, v_cmp)


def _bf_round(x):
    return x.astype(BF16).astype(F32)


def _fox_sample_body(pt_ref, q_ref, kn_ref, vn_ref, ck_ref, cn_ref, cq_ref, *refs, n_pages):
    k_refs, v_refs, o_ref = refs[:n_pages], refs[n_pages:2 * n_pages], refs[2 * n_pages]
    R = q_ref.shape[1]
    H = FOX_HEADS
    scale = HEAD_DIM ** -0.5
    q = q_ref[0].astype(BF16)
    cq = cq_ref[0]
    r_iota = lax.broadcasted_iota(jnp.int32, (R, 1), 0)
    width = PAGE_SIZE * H
    same_head = (lax.broadcasted_iota(jnp.int32, (1, width), 1) % H) == (r_iota % H)
    segs = []
    for p in range(n_pages):
        s = lax.dot_general(q, k_refs[p][0].astype(BF16), NT, preferred_element_type=F32) * scale
        segs.append(jnp.where(same_head, s + cq - ck_ref[0, :, p * width:(p + 1) * width], NEG))
    l_new = lax.broadcasted_iota(jnp.int32, (1, kn_ref.shape[1]), 1)
    new_ok = ((l_new % H) == (r_iota % H)) & ((l_new // H) <= (r_iota // H))
    s = lax.dot_general(q, kn_ref[0].astype(BF16), NT, preferred_element_type=F32) * scale
    segs.append(jnp.where(new_ok, s + cq - cn_ref[0], NEG))
    m = functools.reduce(jnp.maximum, [jnp.max(s, axis=-1, keepdims=True) for s in segs])
    es = [jnp.exp(s - m) for s in segs]
    inv = 1.0 / functools.reduce(jnp.add, [jnp.sum(e, axis=-1, keepdims=True) for e in es])
    acc = jnp.dot((es[n_pages] * inv).astype(BF16), vn_ref[0].astype(BF16), preferred_element_type=F32)
    for p in range(n_pages):
        acc = acc + jnp.dot((es[p] * inv).astype(BF16), v_refs[p][0].astype(BF16), preferred_element_type=F32)
    o_ref[0] = acc


def _page_spec(p, rows):
    return pl.BlockSpec((1, rows, HEAD_DIM), lambda b, pt, p=p: (pt[b, p], 0, 0))


def fox_sample_attend(q_th, k_new, v_new, ck, cn, cq, page_table, k_pool, v_pool):
    B, R, D = q_th.shape
    n_pages = page_table.shape[1]
    rows = PAGE_SIZE * FOX_HEADS
    per_b = lambda shape: pl.BlockSpec((1,) + shape, lambda b, pt: (b, 0, 0))
    grid_spec = pltpu.PrefetchScalarGridSpec(
        num_scalar_prefetch=1, grid=(B,),
        in_specs=[per_b((R, D)), per_b(k_new.shape[1:]), per_b(v_new.shape[1:]), per_b(ck.shape[1:]),
                  per_b(cn.shape[1:]), per_b((R, 1))]
                 + [_page_spec(p, rows) for p in range(n_pages)]
                 + [_page_spec(p, rows) for p in range(n_pages)],
        out_specs=per_b((R, D)))
    return pl.pallas_call(
        functools.partial(_fox_sample_body, n_pages=n_pages),
        grid_spec=grid_spec,
        out_shape=jax.ShapeDtypeStruct((B, R, D), F32),
        compiler_params=pltpu.CompilerParams(dimension_semantics=("arbitrary",), vmem_limit_bytes=VMEM_LIMIT),
        name="fox_sample",
    )(page_table, q_th, k_new, v_new, ck, cn, cq, *([k_pool] * n_pages), *([v_pool] * n_pages))


def _nsa_sample_body(pt_ref, q_ref, gt_ref, ksn_ref, vsn_ref, kwn_ref, vwn_ref, wk_ref, wv_ref,
                     posk_ref, posv_ref, phik_ref, phiv_ref, expand_ref, *refs, n_pages, T):
    kc_refs = refs[:n_pages]
    vc_refs = refs[n_pages:2 * n_pages]
    ks_refs = refs[2 * n_pages:3 * n_pages]
    vs_refs = refs[3 * n_pages:4 * n_pages]
    o_ref = refs[4 * n_pages]
    G = NSA_KV_HEADS
    past = n_pages * PAGE_SIZE
    win_buf = wk_ref.shape[1] // G
    scale = HEAD_DIM ** -0.5
    R = NSA_GROUP * T
    chunks = PAGE_SIZE // CMP_STRIDE

    def group_rows(ref, g, n):
        return ref[pl.ds(0, 1), pl.ds(g, n, stride=G), :][0]

    def compress_pages(page_refs, pos_ref, phi_ref, g):
        first, second = [], []
        for p in range(n_pages):
            x = group_rows(page_refs[p], g, PAGE_SIZE)
            first.append((x * pos_ref[0]).reshape(chunks, CMP_STRIDE, HEAD_DIM).sum(axis=1))
            second.append((x * pos_ref[1]).reshape(chunks, CMP_STRIDE, HEAD_DIM).sum(axis=1))
        first = jnp.concatenate(first, axis=0)
        second = jnp.concatenate(second, axis=0)
        pooled = first + pltpu.roll(second, first.shape[0] - 1, axis=0)
        return jnp.dot(pooled.astype(BF16), phi_ref[...], preferred_element_type=F32).astype(BF16)

    r_iota = lax.broadcasted_iota(jnp.int32, (R, 1), 0)
    t_of_r = r_iota % T
    tq = past + t_of_r
    n_iota = lax.broadcasted_iota(jnp.int32, (1, 128), 1)
    t_cmp = n_iota * CMP_STRIDE + (CMP_BLOCK - 1)
    dist_c = (tq - t_cmp).astype(F32)
    mask_c = t_cmp <= tq
    hs_r = lax.broadcasted_iota(jnp.int32, (128, R), 0)
    hs_c = lax.broadcasted_iota(jnp.int32, (128, R), 1)
    head_sum = ((hs_r < R) & (hs_c % T == hs_r % T)).astype(BF16)
    t_lane = past + lax.broadcasted_iota(jnp.int32, (1, 128), 1) % T
    n_slc = -(-(past + T) // SEL_BLOCK)
    n_slc_pad = -(-n_slc // 8) * 8
    new_blk = past // SEL_BLOCK
    kpos = lax.broadcasted_iota(jnp.int32, (1, past), 1)
    dist_s = (tq - kpos).astype(F32)
    tw = past - win_buf + lax.broadcasted_iota(jnp.int32, (1, win_buf), 1)
    dwin = tq - tw
    mask_w = (dwin >= 0) & (dwin < WINDOW) & (tw >= 0)
    dwin_f = dwin.astype(F32)

    def softmax_parts(s_past, mask_past, new_scores):
        s_past = jnp.where(mask_past, s_past, NEG)
        m = jnp.max(s_past, axis=-1, keepdims=True)
        for sj in new_scores:
            m = jnp.maximum(m, sj)
        e_past = jnp.where(mask_past, jnp.exp(s_past - m), 0.0)
        e_new = [jnp.where(sj > 0.5 * NEG, jnp.exp(sj - m), 0.0) for sj in new_scores]
        denom = jnp.sum(e_past, axis=-1, keepdims=True)
        for ej in e_new:
            denom = denom + ej
        return (e_past / denom).astype(BF16), [ej / denom for ej in e_new]

    def new_scores_of(g, kn_ref, qf, slope):
        out = []
        for j in range(T):
            kj = _bf_round(kn_ref[0, j * G + g:j * G + g + 1, :])
            out.append(jnp.sum(qf * kj, axis=-1, keepdims=True) * scale - slope * (t_of_r - j).astype(F32))
        return out

    def add_new_values(acc, g, p_new, vn_ref):
        for j in range(T):
            acc = acc + _bf_round(p_new[j]) * _bf_round(vn_ref[0, j * G + g:j * G + g + 1, :])
        return acc

    groups = range(G)
    rows = [slice(g * R, (g + 1) * R) for g in groups]
    slopes = []
    for g in groups:
        slope = jnp.zeros((R, 1), F32)
        for h in range(NSA_GROUP):
            slope = jnp.where(r_iota // T == h, 2.0 ** -(g * NSA_GROUP + h + 1), slope)
        slopes.append(slope)
    qs = [q_ref[0, rows[g], :].astype(BF16) for g in groups]
    qfs = [q.astype(F32) for q in qs]
    k_cmp = [compress_pages(kc_refs, posk_ref, phik_ref, g) for g in groups]
    v_cmp = [compress_pages(vc_refs, posv_ref, phiv_ref, g) for g in groups]

    s_cmp = [lax.dot_general(qs[g], k_cmp[g], NT, preferred_element_type=F32) * scale - slopes[g] * dist_c
             for g in groups]
    s_win = [lax.dot_general(qs[g], group_rows(wk_ref, g, win_buf).astype(BF16), NT,
                             preferred_element_type=F32) * scale - slopes[g] * dwin_f for g in groups]
    s_sel = [jnp.concatenate(
        [lax.dot_general(qs[g], group_rows(ks_refs[p], g, PAGE_SIZE).astype(BF16), NT, preferred_element_type=F32)
         for p in range(n_pages)], axis=1) * scale - slopes[g] * dist_s for g in groups]
    new_win = [new_scores_of(g, kwn_ref, qfs[g], slopes[g]) for g in groups]
    new_sel = [new_scores_of(g, ksn_ref, qfs[g], slopes[g]) for g in groups]

    p_cmp = [_masked_softmax_rows(s_cmp[g], mask_c) for g in groups]
    o_cmp = [jnp.dot(p_cmp[g].astype(BF16), v_cmp[g], preferred_element_type=F32) for g in groups]
    p_sum = [sum(jnp.dot(head_sum, x, preferred_element_type=F32) for x in _split3(p_cmp[g])) for g in groups]

    pw = [softmax_parts(s_win[g], mask_w, [jnp.where(t_of_r >= j, new_win[g][j], NEG) for j in range(T)])
          for g in groups]
    o_win = [add_new_values(jnp.dot(pw[g][0], group_rows(wv_ref, g, win_buf).astype(BF16), preferred_element_type=F32),
                            g, pw[g][1], vwn_ref) for g in groups]

    picked_all = _select_blocks(jnp.concatenate(p_sum, axis=0), jnp.concatenate([t_lane] * G, axis=1),
                                n_slc_pad, min(N_SEL, n_slc))
    picked = [picked_all[g * 128:g * 128 + R] for g in groups]

    sel_mask = [jnp.dot(picked[g].astype(BF16), expand_ref[...], preferred_element_type=F32) > 0.5 for g in groups]
    ps = [softmax_parts(s_sel[g], sel_mask[g],
                        [jnp.where((picked[g][:, new_blk:new_blk + 1] > 0.5) & (t_of_r >= j), new_sel[g][j], NEG)
                         for j in range(T)]) for g in groups]
    o_sel = []
    for g in groups:
        acc = jnp.zeros((R, HEAD_DIM), F32)
        for p in range(n_pages):
            acc = acc + jnp.dot(ps[g][0][:, p * PAGE_SIZE:(p + 1) * PAGE_SIZE],
                                group_rows(vs_refs[p], g, PAGE_SIZE).astype(BF16), preferred_element_type=F32)
        o_sel.append(add_new_values(acc, g, ps[g][1], vsn_ref))

    for g in groups:
        gts = gt_ref[0, rows[g], :]
        o_ref[0, rows[g], :] = gts[:, 0:1] * o_cmp[g] + gts[:, 1:2] * o_sel[g] + gts[:, 2:3] * o_win[g]


def nsa_sample_attend(nq_ht, gates_ht, ks_new, vs_new, kw_new, vw_new, win_k, win_v, pos_k2, pos_v2, phi_k, phi_v,
                      page_table, ck_pool, cv_pool, sk_pool, sv_pool):
    B, RT, D = nq_ht.shape
    T = RT // NSA_HEADS
    G = NSA_KV_HEADS
    n_pages = page_table.shape[1]
    past = n_pages * PAGE_SIZE
    expand = (jnp.arange(past)[None, :] // SEL_BLOCK == jnp.arange(128)[:, None]).astype(BF16)
    assert (n_pages * PAGE_SIZE + T) // CMP_STRIDE == n_pages * PAGE_SIZE // CMP_STRIDE and T <= SEL_BLOCK
    per_b = lambda shape: pl.BlockSpec((1,) + shape, lambda b, pt: (b, 0, 0))
    const = lambda shape: pl.BlockSpec(shape, lambda b, pt: (0,) * len(shape))
    grid_spec = pltpu.PrefetchScalarGridSpec(
        num_scalar_prefetch=1, grid=(B,),
        in_specs=[per_b((RT, D)), per_b((RT, 3))] + [per_b((T * G, D))] * 4 + [per_b(win_k.shape[1:])] * 2
                 + [const((2, PAGE_SIZE, D))] * 2 + [const((HEAD_DIM, HEAD_DIM))] * 2 + [const(expand.shape)]
                 + [_page_spec(p, PAGE_SIZE * G) for p in range(n_pages)] * 4,
        out_specs=per_b((RT, D)))
    pools = [ck_pool] * n_pages + [cv_pool] * n_pages + [sk_pool] * n_pages + [sv_pool] * n_pages
    return pl.pallas_call(
        functools.partial(_nsa_sample_body, n_pages=n_pages, T=T),
        grid_spec=grid_spec,
        out_shape=jax.ShapeDtypeStruct((B, RT, D), F32),
        compiler_params=pltpu.CompilerParams(dimension_semantics=("arbitrary",), vmem_limit_bytes=VMEM_LIMIT),
        name="nsa_sample",
    )(page_table, nq_ht, gates_ht, ks_new, vs_new, kw_new, vw_new, win_k, win_v, pos_k2, pos_v2, phi_k, phi_v,
      expand, *pools)


def _tile_pos(w_pos):
    halves = w_pos.reshape(2, CMP_STRIDE, HEAD_DIM)
    return jnp.tile(halves, (1, PAGE_SIZE // CMP_STRIDE, 1))


def mixers_sample(fq, fk, fv, lf, nq, gates, kc, vc, ks, vs, kw, vw, win_k, win_v, page_table,
                  fox_k_pool, fox_v_pool, fox_lf_pool, cmp_k_pool, cmp_v_pool, slc_k_pool, slc_v_pool,
                  cmp_pos_k, cmp_phi_k, cmp_pos_v, cmp_phi_v):
    B, T = fq.shape[:2]
    n_phys = fox_k_pool.shape[0]
    past = page_table.shape[1] * PAGE_SIZE
    H = FOX_HEADS
    assert T * H <= HEAD_DIM
    lf_past = fox_lf_pool[page_table].reshape(B, past, H)
    c = jnp.cumsum(jnp.concatenate([lf_past, lf], axis=1), axis=1)
    c_new = c[:, past:].reshape(B, 1, T * H)
    pad_rows = lambda a: jnp.pad(a.reshape(B, T * H, HEAD_DIM), ((0, 0), (0, HEAD_DIM - T * H), (0, 0)))
    fox_o = fox_sample_attend(fq.reshape(B, T * H, HEAD_DIM), pad_rows(fk), pad_rows(fv),
                              c[:, :past].reshape(B, 1, past * H),
                              jnp.pad(c_new, ((0, 0), (0, 0), (0, HEAD_DIM - T * H))), c_new.reshape(B, T * H, 1),
                              page_table, fox_k_pool.reshape(n_phys, PAGE_SIZE * H, HEAD_DIM),
                              fox_v_pool.reshape(n_phys, PAGE_SIZE * H, HEAD_DIM)).reshape(B, T, FOX_W)
    ht = lambda a: jnp.swapaxes(a, 1, 2).reshape(B, NSA_HEADS * T, a.shape[-1])
    flat = lambda a: a.reshape(a.shape[0], a.shape[1] * NSA_KV_HEADS, HEAD_DIM)
    nsa_ht = nsa_sample_attend(ht(nq), ht(gates), flat(ks), flat(vs), flat(kw), flat(vw), flat(win_k), flat(win_v),
                               _tile_pos(cmp_pos_k), _tile_pos(cmp_pos_v), cmp_phi_k.astype(BF16), cmp_phi_v.astype(BF16),
                               page_table, flat(cmp_k_pool), flat(cmp_v_pool), flat(slc_k_pool), flat(slc_v_pool))
    nsa_o = jnp.swapaxes(nsa_ht.reshape(B, NSA_HEADS, T, HEAD_DIM), 1, 2).reshape(B, T, NSA_W)
    return fox_o, nsa_o


def merge_mixers(x, fox_o, nsa_o, g_fox_out, g_nsa_out, w_out_bf):
    B, T, D = x.shape
    n = B * T
    y = project([(fox_o.reshape(n, FOX_W), g_fox_out, w_out_bf[:FOX_W]),
                 (nsa_o.reshape(n, NSA_W), g_nsa_out, w_out_bf[FOX_W:])], res=x.reshape(n, D))
    return y.reshape(B, T, D)


def memory_kv(mem, g, w_mkv_bf):
    B, M, D = mem.shape
    kv = project([(mem.reshape(B * M, D), g, w_mkv_bf)])
    return (kv[:, :MEM_W].reshape(B, M, MEM_HEADS, HEAD_DIM), kv[:, MEM_W:].reshape(B, M, MEM_HEADS, HEAD_DIM))


def _mem_attend_body(q_ref, k_ref, v_ref, o_ref):
    H = MEM_HEADS
    q = q_ref[0].astype(BF16)
    s = lax.dot_general(q, k_ref[0].astype(BF16), NT, preferred_element_type=F32) * HEAD_DIM ** -0.5
    r_iota = lax.broadcasted_iota(jnp.int32, (s.shape[0], 1), 0)
    c_iota = lax.broadcasted_iota(jnp.int32, (1, s.shape[1]), 1)
    s = jnp.where((c_iota % H) == (r_iota % H), s, NEG)
    e = jnp.exp(s - jnp.max(s, axis=-1, keepdims=True))
    p = e / jnp.sum(e, axis=-1, keepdims=True)
    o_ref[0] = jnp.dot(p.astype(BF16), v_ref[0].astype(BF16), preferred_element_type=F32)


def memory_attend(x, g, w_mq_bf, w_mo_bf, mk, mv):
    B, T, D = x.shape
    M, H = mk.shape[1], MEM_HEADS
    x2 = x.reshape(B * T, D)
    q = project([(x2, g, w_mq_bf)]).reshape(B, T * H, HEAD_DIM)
    rows = min(MEM_ROW_TILE, T * H)
    kv_spec = pl.BlockSpec((1, M * H, HEAD_DIM), lambda b, i: (b, 0, 0))
    o = pl.pallas_call(
        _mem_attend_body,
        grid=(B, T * H // rows),
        in_specs=[pl.BlockSpec((1, rows, HEAD_DIM), lambda b, i: (b, i, 0)), kv_spec, kv_spec],
        out_specs=pl.BlockSpec((1, rows, HEAD_DIM), lambda b, i: (b, i, 0)),
        out_shape=jax.ShapeDtypeStruct((B, T * H, HEAD_DIM), F32),
        compiler_params=pltpu.CompilerParams(dimension_semantics=("arbitrary", "arbitrary"),
                                             vmem_limit_bytes=VMEM_LIMIT),
        name="mem_attend",
    )(q, mk.reshape(B, M * H, HEAD_DIM), mv.reshape(B, M * H, HEAD_DIM))
    return project([(o.reshape(B * T, MEM_W), None, w_mo_bf)], res=x2).reshape(B, T, D)


def _top16_rows(s):
    iota = lax.broadcasted_iota(jnp.int32, s.shape, 0)
    vals = []
    for _ in range(PEER_TOPK):
        m = jnp.max(s, axis=0, keepdims=True)
        idx = jnp.min(jnp.where(s == m, iota, PEER_KEYS), axis=0, keepdims=True)
        s = jnp.where(iota == idx, NEG_INF, s)
        vals.append(m)
    return s == NEG_INF, jnp.concatenate(vals, axis=0)


def _peer_scores_body(x_ref, g_ref, wq_ref, k1_ref, k2_ref, xn_ref, a_ref, b_ref, tau_ref):
    h = pl.program_id(1)

    @pl.when(h == 0)
    def _():
        xf = x_ref[...]
        y = xf * lax.rsqrt(jnp.mean(xf * xf, axis=-1, keepdims=True) + EPS) * g_ref[...]
        xn_ref[...] = y.astype(BF16)

    qT = lax.dot_general(wq_ref[...], xn_ref[...], (((1,), (1,)), ((), ())), preferred_element_type=F32)
    half = qT.shape[0] // 2
    s1 = jnp.dot(k1_ref[...], qT[:half].astype(BF16), preferred_element_type=F32)
    s2 = jnp.dot(k2_ref[...], qT[half:].astype(BF16), preferred_element_type=F32)
    mem1, v1 = _top16_rows(s1)
    mem2, v2 = _top16_rows(s2)
    pieces = [v1[0:1] + v2]
    for r in range(1, 8):
        pieces.append(v1[r:r + 1] + v2[0:8])
    pieces.append(v1[8:16] + v2[0:1])
    c = jnp.concatenate(pieces, axis=0)
    rank = jnp.zeros(c.shape, F32)
    for j in range(c.shape[0]):
        rank = rank + jnp.where(c[j:j + 1] > c, 1.0, 0.0)
    c16 = jnp.min(jnp.where(rank <= PEER_TOPK - 1, c, jnp.inf), axis=0, keepdims=True)
    c17 = jnp.min(jnp.where(rank <= PEER_TOPK, c, jnp.inf), axis=0, keepdims=True)
    z = jnp.sum(jnp.where(c >= c16, jnp.exp(c - c[0:1]), 0.0), axis=0, keepdims=True)
    shift = v1[0:1] + jnp.log(z)
    log2e = math.log2(math.e)
    a_ref[0] = jnp.where(mem1, (s1 - shift) * log2e, NEG_INF)
    b_ref[0] = jnp.where(mem2, (s2 - v2[0:1]) * log2e, NEG_INF)
    tau_ref[0] = (0.5 * (c16 + c17) - shift - v2[0:1]) * log2e


def peer_scores(x2, g, wqT_bf, k1_bf, k2_bf, tt):
    n, d = x2.shape
    H = PEER_HEADS
    qd = wqT_bf.shape[0] // H
    return pl.pallas_call(
        _peer_scores_body,
        grid=(n // tt, H),
        in_specs=[pl.BlockSpec((tt, d), lambda i, h: (i, 0)),
                  pl.BlockSpec((1, d), lambda i, h: (0, 0)),
                  pl.BlockSpec((qd, d), lambda i, h: (h, 0)),
                  pl.BlockSpec((PEER_KEYS, qd // 2), lambda i, h: (0, 0)),
                  pl.BlockSpec((PEER_KEYS, qd // 2), lambda i, h: (0, 0))],
        out_specs=[pl.BlockSpec((tt, d), lambda i, h: (i, 0))]
                  + [pl.BlockSpec((1, PEER_KEYS, tt), lambda i, h: (h, 0, i))] * 2
                  + [pl.BlockSpec((1, 1, tt), lambda i, h: (h, 0, i))],
        out_shape=[jax.ShapeDtypeStruct((n, d), BF16)]
                  + [jax.ShapeDtypeStruct((H, PEER_KEYS, n), F32)] * 2
                  + [jax.ShapeDtypeStruct((H, 1, n), F32)],
        compiler_params=pltpu.CompilerParams(dimension_semantics=("arbitrary", "arbitrary"),
                                             vmem_limit_bytes=VMEM_LIMIT),
        name="peer_scores",
    )(x2, g.reshape(1, d), wqT_bf, k1_bf, k2_bf)


def _gelu_exact(x):
    return 0.5 * x * (1.0 + lax.erf(x * (2.0 ** -0.5)))


def _peer_dense_body(x_ref, gf_ref, xn_ref, u_ref, vt_ref, a_ref, b_ref, tau_ref, o_ref, acc_ref, h_ref, p_ref,
                     *, te, final_norm):
    j = pl.program_id(1)

    @pl.when(j == 0)
    def _():
        acc_ref[...] = jnp.zeros(acc_ref.shape, F32)

    na = te // PEER_KEYS
    lane_tile = PEER_KEYS
    for k in range(PEER_SUBTILES):
        sub = slice(k * te // PEER_SUBTILES, (k + 1) * te // PEER_SUBTILES)
        h_ref[sub, :] = lax.dot_general(u_ref[sub, :], xn_ref[...], NT, preferred_element_type=F32)
    for k in range(PEER_SUBTILES):
        sub = slice(k * te // PEER_SUBTILES, (k + 1) * te // PEER_SUBTILES)
        for al in range(k * na // PEER_SUBTILES, (k + 1) * na // PEER_SUBTILES):
            a = j * na + al
            rows = slice(al * PEER_KEYS, (al + 1) * PEER_KEYS)
            a_rows = [a_ref[h, pl.ds(a, 1), :] for h in range(PEER_HEADS)]
            for c in range(h_ref.shape[1] // lane_tile):
                lanes = slice(c * lane_tile, (c + 1) * lane_tile)
                w = jnp.zeros((PEER_KEYS, lane_tile), F32)
                for h in range(PEER_HEADS):
                    t = a_rows[h][:, lanes] + b_ref[h, :, lanes]
                    w = w + jnp.where(t >= tau_ref[h, :, lanes], jnp.exp2(t), 0.0)
                p_ref[rows, lanes] = (w * _gelu_exact(h_ref[rows, lanes])).astype(BF16)
        acc_ref[...] += jnp.dot(vt_ref[0, :, sub], p_ref[sub, :], preferred_element_type=F32)

    @pl.when(j == pl.num_programs(1) - 1)
    def _():
        y = x_ref[...] + acc_ref[...].T
        if final_norm:
            y = y * lax.rsqrt(jnp.mean(y * y, axis=-1, keepdims=True) + EPS) * gf_ref[...]
        o_ref[...] = y


def expert_tiles_transposed(v):
    ne, d = v.shape
    return jnp.swapaxes(v.reshape(ne // PEER_EXPERT_TILE, PEER_EXPERT_TILE, d), 1, 2).astype(BF16)


def peer_dense(x2, g_final, xn_bf, u_bf, vT_bf, a, b, tau, tt, te, final_norm):
    n, d = xn_bf.shape
    ne = u_bf.shape[0]
    assert vT_bf.shape == (ne // te, d, te)
    H = PEER_HEADS
    return pl.pallas_call(
        functools.partial(_peer_dense_body, te=te, final_norm=final_norm),
        grid=(n // tt, ne // te),
        in_specs=[pl.BlockSpec((tt, d), lambda i, j: (i, 0), pipeline_mode=pl.Buffered(1)),
                  pl.BlockSpec((1, d), lambda i, j: (0, 0)),
                  pl.BlockSpec((tt, d), lambda i, j: (i, 0)),
                  pl.BlockSpec((te, d), lambda i, j: (j, 0)),
                  pl.BlockSpec((1, d, te), lambda i, j: (j, 0, 0))]
                 + [pl.BlockSpec((H, PEER_KEYS, tt), lambda i, j: (0, 0, i), pipeline_mode=pl.Buffered(1))] * 2
                 + [pl.BlockSpec((H, 1, tt), lambda i, j: (0, 0, i))],
        out_specs=pl.BlockSpec((tt, d), lambda i, j: (i, 0)),
        out_shape=jax.ShapeDtypeStruct((n, d), F32),
        scratch_shapes=[pltpu.VMEM((d, tt), F32), pltpu.VMEM((te, tt), F32), pltpu.VMEM((te, tt), BF16)],
        compiler_params=pltpu.CompilerParams(dimension_semantics=("arbitrary", "arbitrary"),
                                             vmem_limit_bytes=VMEM_LIMIT),
        name="peer_dense",
    )(x2, g_final.reshape(1, d), xn_bf, u_bf, vT_bf, a, b, tau)


def peer_ffn(x, g, wqT_bf, k1_bf, k2_bf, u_bf, vT_bf, g_final, final_norm):
    B, T, D = x.shape
    x2 = x.reshape(B * T, D)
    tt = min(PEER_TOKEN_TILE, B * T)
    assert (B * T) % tt == 0
    xn_bf, a, b, tau = peer_scores(x2, g, wqT_bf, k1_bf, k2_bf, tt)
    out = peer_dense(x2, g_final, xn_bf, u_bf, vT_bf, a, b, tau, tt, PEER_EXPERT_TILE, final_norm)
    return out.reshape(B, T, D)


def kernel(x_prompt, x_sample, mem_prompt, cache_fox_k, cache_fox_v, cache_fox_logf,
           cache_cmp_k, cache_cmp_v, cache_slc_k, cache_slc_v, state_win_k, state_win_v,
           cache_mem_k, cache_mem_v, page_table,
           g_mix, w_in, b_fox_f, b_nsa_gate, cmp_pos_k, cmp_phi_k, cmp_pos_v, cmp_phi_v,
           g_fox_out, g_nsa_out, w_out, g_mem_q, g_mem_kv, w_mq, w_mk, w_mv, w_mo,
           g_peer, w_pq, peer_subkey_1, peer_subkey_2, peer_u, peer_v, g_final):
    depth = g_mix.shape[0]
    win_buf = min(WINDOW, PAST_LEN)
    names_p = ('fox_k', 'fox_v', 'fox_logf', 'cmp_k', 'cmp_v', 'slc_k', 'slc_v', 'win_k', 'win_v', 'mem_k', 'mem_v')
    names_s = ('fox_k', 'fox_v', 'fox_logf', 'cmp_k', 'cmp_v', 'slc_k', 'slc_v', 'win_k', 'win_v')
    sp = {n: [] for n in names_p}
    ss = {n: [] for n in names_s}
    xp, xs = x_prompt, x_sample
    for l in range(depth):
        w_in_bf = regroup_w_in(w_in[l])
        w_out_bf, w_mq_bf, w_mo_bf = w_out[l].astype(BF16), w_mq[l].astype(BF16), w_mo[l].astype(BF16)
        w_mkv_bf = jnp.concatenate([w_mk[l], w_mv[l]], axis=1).astype(BF16)
        peer_w = (w_pq[l].T.astype(BF16), peer_subkey_1[l].astype(BF16), peer_subkey_2[l].astype(BF16),
                  peer_u[l].astype(BF16), expert_tiles_transposed(peer_v[l]))

        (fq, fk, fv, lf, nq, kc, vc, ks, vs, kw, vw, gt), packed = project_mixers(
            xp, g_mix[l], w_in_bf, b_fox_f[l], b_nsa_gate[l], bf16_copy=True)
        fox_o = fox_prompt(packed, lf)
        nsa_o = nsa_prompt(packed, gt, kc, vc, cmp_pos_k[l], cmp_phi_k[l], cmp_pos_v[l], cmp_phi_v[l])
        xp = merge_mixers(xp, fox_o, nsa_o, g_fox_out[l], g_nsa_out[l], w_out_bf)
        mk, mv = memory_kv(mem_prompt, g_mem_kv[l], w_mkv_bf)
        xp = memory_attend(xp, g_mem_q[l], w_mq_bf, w_mo_bf, mk, mv)
        last = l == depth - 1
        xp = peer_ffn(xp, g_peer[l], *peer_w, g_final, last)
        for name, val in zip(names_p, (fk, fv, lf, kc, vc, ks, vs, last_rows(kw, win_buf), last_rows(vw, win_buf), mk, mv)):
            sp[name].append(val)

        (fq, fk, fv, lf, nq, kc, vc, ks, vs, kw, vw, gt), _ = project_mixers(
            xs, g_mix[l], w_in_bf, b_fox_f[l], b_nsa_gate[l])
        kw_all = jnp.concatenate([state_win_k[l].astype(kw.dtype), kw], axis=1)
        vw_all = jnp.concatenate([state_win_v[l].astype(vw.dtype), vw], axis=1)
        fox_o, nsa_o = mixers_sample(fq, fk, fv, lf, nq, gt, kc, vc, ks, vs, kw, vw,
                                     state_win_k[l], state_win_v[l], page_table,
                                     cache_fox_k[l], cache_fox_v[l], cache_fox_logf[l],
                                     cache_cmp_k[l], cache_cmp_v[l], cache_slc_k[l], cache_slc_v[l],
                                     cmp_pos_k[l], cmp_phi_k[l], cmp_pos_v[l], cmp_phi_v[l])
        xs = merge_mixers(xs, fox_o, nsa_o, g_fox_out[l], g_nsa_out[l], w_out_bf)
        xs = memory_attend(xs, g_mem_q[l], w_mq_bf, w_mo_bf, cache_mem_k[l], cache_mem_v[l])
        xs = peer_ffn(xs, g_peer[l], *peer_w, g_final, last)
        for name, val in zip(names_s, (fk, fv, lf, kc, vc, ks, vs, kw_all[:, -win_buf:], vw_all[:, -win_buf:])):
            ss[name].append(val)

    y_prompt, y_sample = (xp, xs) if depth else (rmsnorm_pallas(xp, g_final), rmsnorm_pallas(xs, g_final))
    sp = {n: jnp.stack(v) for n, v in sp.items()}
    ss = {n: jnp.stack(v) for n, v in ss.items()}
    return (y_prompt, y_sample,
            sp['fox_k'], sp['fox_v'], sp['fox_logf'], sp['cmp_k'], sp['cmp_v'], sp['slc_k'], sp['slc_v'],
            sp['win_k'], sp['win_v'], sp['mem_k'], sp['mem_v'],
            ss['fox_k'], ss['fox_v'], ss['fox_logf'], ss['cmp_k'], ss['cmp_v'], ss['slc_k'], ss['slc_v'],
            ss['win_k'], ss['win_v'])
```

```python
import functools
import math
import jax
import jax.numpy as jnp
from jax import lax
import numpy as np
from jax.experimental import pallas as pl
from jax.experimental.pallas import tpu as pltpu

PAST_LEN = 2048
PAGE_SIZE = 128

HEAD_DIM = 128
FOX_HEADS = 8
NSA_HEADS = 8
NSA_KV_HEADS = 2
NSA_GROUP = NSA_HEADS // NSA_KV_HEADS
CMP_STRIDE = 16
CMP_BLOCK = 2 * CMP_STRIDE
SEL_BLOCK = 64
N_SEL = 16
WINDOW = 512
MEM_HEADS = 4
PEER_HEADS = 8
PEER_KEYS = 128
PEER_TOPK = 16
Q_BLOCK = 128
FORCE_BONUS = 1.0e4
NEG = -1.0e30
EPS = 1e-6

FOX_W = FOX_HEADS * HEAD_DIM
NSA_W = NSA_HEADS * HEAD_DIM
KV_W = NSA_KV_HEADS * HEAD_DIM
MEM_W = MEM_HEADS * HEAD_DIM
IN_SIZES = (FOX_W, FOX_W, FOX_W, FOX_HEADS, NSA_W, KV_W, KV_W, KV_W, KV_W, KV_W, KV_W, 3 * NSA_HEADS)
P_IN = sum(IN_SIZES)

F32 = jnp.float32
BF16 = jnp.bfloat16
NEG_INF = float("-inf")
VMEM_LIMIT = 56 * 1024 * 1024
PEER_TOKEN_TILE = 512
PEER_EXPERT_TILE = 1024
PEER_SUBTILES = 2
SLC_CHUNK = 512
FOX_CHUNK = 512
PROJ_ROW_TILE = 1024
PROJ_COL_TILE = 512
PROJ_IN_COL_TILE = 640
MEM_ROW_TILE = 512
NT = (((1,), (1,)), ((), ()))


def _rmsnorm_body(x_ref, g_ref, o_ref):
    xf = x_ref[...]
    y = xf * lax.rsqrt(jnp.mean(xf * xf, axis=-1, keepdims=True) + EPS)
    o_ref[...] = y * g_ref[...]


def rmsnorm_pallas(x, g, rows=256):
    shp = x.shape
    d = shp[-1]
    x2 = x.reshape(-1, d)
    n = x2.shape[0]
    rows = min(rows, n)
    assert n % rows == 0
    out = pl.pallas_call(
        _rmsnorm_body,
        grid=(n // rows,),
        in_specs=[pl.BlockSpec((rows, d), lambda i: (i, 0)),
                  pl.BlockSpec((1, d), lambda i: (0, 0))],
        out_specs=pl.BlockSpec((rows, d), lambda i: (i, 0)),
        out_shape=jax.ShapeDtypeStruct((n, d), F32),
        name="final_rmsnorm",
    )(x2, g.reshape(1, d))
    return out.reshape(shp)


def last_rows(a, n):
    T = a.shape[1]
    if T < n:
        a = jnp.pad(a, ((0, 0), (n - T, 0)) + ((0, 0),) * (a.ndim - 2))
    return a[:, -n:]


def _proj_body(*refs, n_in, normed, has_res, bf16_copy):
    n_refs_in = n_in * (3 if normed else 2) + (1 if has_res else 0)
    o_ref = refs[n_refs_in]
    n_out = 2 if bf16_copy else 1
    xn_refs = refs[n_refs_in + n_out:]
    per = 3 if normed else 2

    @pl.when(pl.program_id(1) == 0)
    def _():
        for k in range(n_in):
            x = refs[k * per][...]
            if normed:
                x = x * lax.rsqrt(jnp.mean(x * x, axis=-1, keepdims=True) + EPS) * refs[k * per + 1][...]
            xn_refs[k][...] = x.astype(BF16)

    acc = jnp.dot(xn_refs[0][...], refs[per - 1][...], preferred_element_type=F32)
    for k in range(1, n_in):
        acc = acc + jnp.dot(xn_refs[k][...], refs[k * per + per - 1][...], preferred_element_type=F32)
    if has_res:
        acc = acc + refs[n_in * per][...]
    o_ref[...] = acc
    if bf16_copy:
        refs[n_refs_in + 1][...] = acc.astype(BF16)


def project(inputs, res=None, tn=PROJ_COL_TILE, bf16_copy=False):
    n = inputs[0][0].shape[0]
    N = inputs[0][2].shape[1]
    normed = inputs[0][1] is not None
    tm = min(PROJ_ROW_TILE, n)
    tn = min(tn, N)
    assert n % tm == 0 and N % tn == 0
    args, in_specs, scratch = [], [], []
    for x, g, w in inputs:
        d = x.shape[1]
        args.append(x)
        in_specs.append(pl.BlockSpec((tm, d), lambda i, j: (i, 0)))
        if normed:
            args.append(g.reshape(1, d))
            in_specs.append(pl.BlockSpec((1, d), lambda i, j: (0, 0)))
        args.append(w)
        in_specs.append(pl.BlockSpec((d, tn), lambda i, j: (0, j)))
        scratch.append(pltpu.VMEM((tm, d), BF16))
    if res is not None:
        args.append(res)
        in_specs.append(pl.BlockSpec((tm, tn), lambda i, j: (i, j)))
    out_spec = pl.BlockSpec((tm, tn), lambda i, j: (i, j))
    out_shape = jax.ShapeDtypeStruct((n, N), F32)
    return pl.pallas_call(
        functools.partial(_proj_body, n_in=len(inputs), normed=normed, has_res=res is not None, bf16_copy=bf16_copy),
        grid=(n // tm, N // tn),
        in_specs=in_specs,
        out_specs=[out_spec, out_spec] if bf16_copy else out_spec,
        out_shape=[out_shape, jax.ShapeDtypeStruct((n, N), BF16)] if bf16_copy else out_shape,
        scratch_shapes=scratch,
        compiler_params=pltpu.CompilerParams(dimension_semantics=("arbitrary", "arbitrary"),
                                             vmem_limit_bytes=VMEM_LIMIT),
        name="project",
    )(*args)


_IN_OFFS = [int(o) for o in np.cumsum((0,) + IN_SIZES)]
_IN_ORDER = (0, 1, 2, 4, 5, 6, 7, 8, 9, 10, 3, 11)
P_IN_PAD = -(-P_IN // PROJ_IN_COL_TILE) * PROJ_IN_COL_TILE


def regroup_w_in(w_in):
    cols = [w_in[:, _IN_OFFS[k]:_IN_OFFS[k + 1]] for k in _IN_ORDER]
    cols.append(jnp.zeros((w_in.shape[0], P_IN_PAD - P_IN), w_in.dtype))
    return jnp.concatenate(cols, axis=1).astype(BF16)


def _regrouped_cols():
    cols, off = {}, 0
    for k in _IN_ORDER:
        cols[k] = off
        off += IN_SIZES[k]
    return cols


_IN_COL = _regrouped_cols()


def project_mixers(x, g_mix, w_in_bf, b_fox_f, b_nsa_gate, bf16_copy=False):
    B, T, D = x.shape
    out = project([(x.reshape(B * T, D), g_mix, w_in_bf)], tn=PROJ_IN_COL_TILE, bf16_copy=bf16_copy)
    y, y_bf = out if bf16_copy else (out, None)
    parts = {k: y[:, _IN_COL[k]:_IN_COL[k] + IN_SIZES[k]] for k in _IN_ORDER}
    hd = lambda a: a.reshape(B, T, -1, HEAD_DIM)
    logf = jax.nn.log_sigmoid(parts[3] + b_fox_f.astype(F32)).reshape(B, T, FOX_HEADS)
    gates = jax.nn.sigmoid(parts[11] + b_nsa_gate.astype(F32)).reshape(B, T, NSA_HEADS, 3)
    packed = (y.reshape(B, T, P_IN_PAD), y_bf.reshape(B, T, P_IN_PAD)) if bf16_copy else None
    return (hd(parts[0]), hd(parts[1]), hd(parts[2]), logf, hd(parts[4]), hd(parts[5]), hd(parts[6]),
            hd(parts[7]), hd(parts[8]), hd(parts[9]), hd(parts[10]), gates), packed


def _col_window(rows, width, group, row_block):
    assert _IN_COL[group] % width == 0 and IN_SIZES[group] == width
    col = _IN_COL[group] // width
    if row_block:
        return pl.BlockSpec((1, rows, width), lambda b, i: (b, i, col))
    return pl.BlockSpec((1, rows, width), lambda b, i: (b, 0, col))


def _fox_prompt_body(q_ref, k_ref, v_ref, cq_ref, ck_ref, o_ref, m_ref, l_ref, acc_ref):
    i = pl.program_id(1)
    H = FOX_HEADS
    scale = HEAD_DIM ** -0.5
    t_row = i * Q_BLOCK + lax.broadcasted_iota(jnp.int32, (1, Q_BLOCK, 1), 1)
    n_chunks = (i * Q_BLOCK + Q_BLOCK + FOX_CHUNK - 1) // FOX_CHUNK
    head_cols = [slice(h * HEAD_DIM, (h + 1) * HEAD_DIM) for h in range(H)]
    qs = [q_ref[0, :, cols].astype(BF16) for cols in head_cols]
    cq_all = cq_ref[0]
    cq = jnp.stack([cq_all[:, h:h + 1] for h in range(H)])
    m_ref[...] = jnp.full(m_ref.shape, NEG, F32)
    l_ref[...] = jnp.zeros(l_ref.shape, F32)
    acc_ref[...] = jnp.zeros(acc_ref.shape, F32)

    def chunk(c, carry):
        k0 = pl.multiple_of(c * FOX_CHUNK, FOX_CHUNK)
        kpos = k0 + lax.broadcasted_iota(jnp.int32, (1, 1, FOX_CHUNK), 2)
        mask = kpos <= t_row
        s = jnp.stack([lax.dot_general(qs[h], k_ref[0, pl.ds(k0, FOX_CHUNK), head_cols[h]], NT,
                                       preferred_element_type=F32) for h in range(H)]) * scale
        s = jnp.where(mask, s + cq - ck_ref[0, :, :, pl.ds(k0, FOX_CHUNK)], NEG)
        m_old = m_ref[...]
        m_new = jnp.maximum(m_old, jnp.max(s, axis=-1, keepdims=True))
        alpha = jnp.exp(m_old - m_new)
        p = jnp.where(mask, jnp.exp(s - m_new), 0.0)
        m_ref[...] = m_new
        l_ref[...] = alpha * l_ref[...] + jnp.sum(p, axis=-1, keepdims=True)
        pb = p.astype(BF16)
        pv = jnp.stack([jnp.dot(pb[h], v_ref[0, pl.ds(k0, FOX_CHUNK), head_cols[h]], preferred_element_type=F32)
                        for h in range(H)])
        acc_ref[...] = alpha * acc_ref[...] + pv
        return carry

    lax.fori_loop(0, n_chunks, chunk, 0)
    o = acc_ref[...] / l_ref[...]
    for h in range(H):
        o_ref[0, :, head_cols[h]] = o[h]


def fox_prompt(packed, logf):
    y_bf = packed[1]
    B, S = logf.shape[:2]
    assert S % FOX_CHUNK == 0
    H = FOX_HEADS
    c = jnp.cumsum(logf, axis=1)
    block = lambda width: pl.BlockSpec((1, Q_BLOCK, width), lambda b, i: (b, i, 0))
    return pl.pallas_call(
        _fox_prompt_body,
        grid=(B, S // Q_BLOCK),
        in_specs=[_col_window(Q_BLOCK, FOX_W, 0, True), _col_window(S, FOX_W, 1, False),
                  _col_window(S, FOX_W, 2, False), block(H),
                  pl.BlockSpec((1, H, 1, S), lambda b, i: (b, 0, 0, 0))],
        out_specs=block(FOX_W),
        out_shape=jax.ShapeDtypeStruct((B, S, FOX_W), F32),
        scratch_shapes=[pltpu.VMEM((H, Q_BLOCK, 1), F32), pltpu.VMEM((H, Q_BLOCK, 1), F32),
                        pltpu.VMEM((H, Q_BLOCK, HEAD_DIM), F32)],
        compiler_params=pltpu.CompilerParams(dimension_semantics=("arbitrary", "arbitrary"),
                                             vmem_limit_bytes=VMEM_LIMIT),
        name="fox_prompt",
    )(y_bf, y_bf, y_bf, c, jnp.swapaxes(c, 1, 2).reshape(B, H, 1, S))


def compress(k, w_pos, w_phi):
    B, L, G, D = k.shape
    n_chunk = L // CMP_STRIDE
    c = k[:, :n_chunk * CMP_STRIDE].reshape(B, n_chunk, CMP_STRIDE, G, D)
    blocks = jnp.concatenate([c[:, :-1], c[:, 1:]], axis=2)
    pooled = jnp.einsum('bnlgd,ld->bngd', blocks, w_pos)
    return pooled @ w_phi


def _masked_softmax_rows(s, mask):
    s = jnp.where(mask, s, NEG)
    e = jnp.exp(s - jnp.max(s, axis=-1, keepdims=True))
    p = e / jnp.sum(e, axis=-1, keepdims=True)
    return jnp.where(mask, p, 0.0)


def _split3(x):
    hi = x.astype(BF16)
    r = x - hi.astype(F32)
    mid = r.astype(BF16)
    lo = (r - mid.astype(F32)).astype(BF16)
    return hi, mid, lo


def _select_blocks(p_sum, t_lane, n_slc_pad, k_sel):
    T = p_sum.shape[0]
    s_iota = lax.broadcasted_iota(jnp.int32, (n_slc_pad, 1), 0)
    n_iota = lax.broadcasted_iota(jnp.int32, (1, p_sum.shape[1]), 1)
    ratio = SEL_BLOCK // CMP_STRIDE
    selT = ((n_iota < ratio * s_iota + ratio) & (n_iota > ratio * s_iota - CMP_BLOCK // CMP_STRIDE)).astype(BF16)
    impT = sum(lax.dot_general(selT, x, NT, preferred_element_type=F32) for x in _split3(p_sum))
    curT = t_lane // SEL_BLOCK
    validT = s_iota <= curT
    forcedT = (s_iota == 0) | (s_iota == curT) | (s_iota == curT - 1)
    scoreT = jnp.where(validT, impT + jnp.where(forcedT, FORCE_BONUS, 0.0), NEG)
    rank = jnp.zeros(scoreT.shape, F32)
    for sp in range(n_slc_pad):
        row = scoreT[sp:sp + 1]
        beats = (row > scoreT) | ((row == scoreT) & (s_iota > sp))
        rank = rank + jnp.where(beats, 1.0, 0.0)
    pickedT = jnp.where(validT & (rank < k_sel), 1.0, 0.0)
    pad = jnp.zeros((128 - n_slc_pad, T), F32)
    return jnp.concatenate([pickedT, pad], axis=0).T


def _nsa_prompt_body(q_ref, gt_ref, kcmp_ref, vcmp_ref, ks_ref, vs_ref, kw_ref, vw_ref, expand_ref, o_ref,
                     m_ref, l_ref, acc_ref, *, n_slc):
    i = pl.program_id(1)
    scale = HEAD_DIM ** -0.5
    R = NSA_GROUP * Q_BLOCK
    r_iota = lax.broadcasted_iota(jnp.int32, (R, 1), 0)
    t_row = i * Q_BLOCK + r_iota % Q_BLOCK
    t_lane = i * Q_BLOCK + lax.broadcasted_iota(jnp.int32, (1, Q_BLOCK), 1)
    gts = gt_ref[0]
    n_iota = lax.broadcasted_iota(jnp.int32, (1, 128), 1)
    t_cmp = n_iota * CMP_STRIDE + (CMP_BLOCK - 1)
    dist_c = (t_row - t_cmp).astype(F32)
    mask_c = t_cmp <= t_row
    n_chunks = (i * Q_BLOCK + Q_BLOCK + SLC_CHUNK - 1) // SLC_CHUNK
    w0 = pl.multiple_of(jnp.maximum(i - WINDOW // Q_BLOCK, 0) * Q_BLOCK, Q_BLOCK)
    tw = w0 + lax.broadcasted_iota(jnp.int32, (1, WINDOW + Q_BLOCK), 1)
    dwin = t_row - tw
    mask_w = (dwin >= 0) & (dwin < WINDOW)
    dwin_f = dwin.astype(F32)

    for g in range(NSA_KV_HEADS):
        cols = slice(g * HEAD_DIM, (g + 1) * HEAD_DIM)
        heads = [g * NSA_GROUP + h for h in range(NSA_GROUP)]
        slope = jnp.zeros((R, 1), F32)
        for h, hd in enumerate(heads):
            slope = jnp.where(r_iota // Q_BLOCK == h, 2.0 ** -(hd + 1), slope)
        q = jnp.concatenate([q_ref[0, :, hd * HEAD_DIM:(hd + 1) * HEAD_DIM] for hd in heads], axis=0).astype(BF16)

        s = lax.dot_general(q, kcmp_ref[0, :, cols], NT, preferred_element_type=F32) * scale - slope * dist_c
        p = _masked_softmax_rows(s, mask_c)
        o_cmp = jnp.dot(p.astype(BF16), vcmp_ref[0, :, cols], preferred_element_type=F32)
        p_sum = functools.reduce(jnp.add, [p[h * Q_BLOCK:(h + 1) * Q_BLOCK] for h in range(NSA_GROUP)])
        picked = _select_blocks(p_sum, t_lane, n_slc, min(N_SEL, n_slc)).astype(BF16)
        picked_rows = jnp.concatenate([picked] * NSA_GROUP, axis=0)

        m_ref[...] = jnp.full(m_ref.shape, NEG, F32)
        l_ref[...] = jnp.zeros(l_ref.shape, F32)
        acc_ref[...] = jnp.zeros(acc_ref.shape, F32)

        slope_t = slope * t_row.astype(F32)

        def chunk(c, carry):
            k0 = pl.multiple_of(c * SLC_CHUNK, SLC_CHUNK)
            kpos = k0 + lax.broadcasted_iota(jnp.int32, (1, SLC_CHUNK), 1)
            sel = jnp.dot(picked_rows, expand_ref[:, pl.ds(k0, SLC_CHUNK)], preferred_element_type=F32)
            mask = (sel > 0.5) & (kpos <= t_row)
            s = lax.dot_general(q, ks_ref[0, pl.ds(k0, SLC_CHUNK), cols], NT, preferred_element_type=F32) * scale
            s = jnp.where(mask, s - (slope_t - slope * kpos.astype(F32)), NEG)
            m_old = m_ref[...]
            m_new = jnp.maximum(m_old, jnp.max(s, axis=-1, keepdims=True))
            alpha = jnp.exp(m_old - m_new)
            p = jnp.exp(s - m_new)
            m_ref[...] = m_new
            l_ref[...] = alpha * l_ref[...] + jnp.sum(p, axis=-1, keepdims=True)
            acc_ref[...] = alpha * acc_ref[...] + jnp.dot(p.astype(BF16), vs_ref[0, pl.ds(k0, SLC_CHUNK), cols],
                                                          preferred_element_type=F32)
            return carry

        lax.fori_loop(0, n_chunks, chunk, 0)
        o_sel = acc_ref[...] / l_ref[...]

        s = lax.dot_general(q, kw_ref[0, pl.ds(w0, WINDOW + Q_BLOCK), cols], NT,
                            preferred_element_type=F32) * scale - slope * dwin_f
        p = _masked_softmax_rows(s, mask_w)
        o_win = jnp.dot(p.astype(BF16), vw_ref[0, pl.ds(w0, WINDOW + Q_BLOCK), cols], preferred_element_type=F32)

        gate = lambda j: jnp.concatenate([gts[:, 3 * hd + j:3 * hd + j + 1] for hd in heads], axis=0)
        o = gate(0) * o_cmp + gate(1) * o_sel + gate(2) * o_win
        for h, hd in enumerate(heads):
            o_ref[0, :, hd * HEAD_DIM:(hd + 1) * HEAD_DIM] = o[h * Q_BLOCK:(h + 1) * Q_BLOCK]


def nsa_prompt_attend(y_bf, gates, k_cmp, v_cmp):
    B, S = gates.shape[:2]
    W = NSA_W
    assert S % SLC_CHUNK == 0 and S >= WINDOW + Q_BLOCK and S // SEL_BLOCK <= 128
    n_slc = S // SEL_BLOCK
    expand = (jnp.arange(S)[None, :] // SEL_BLOCK == jnp.arange(128)[:, None]).astype(BF16)
    rows = NSA_GROUP * Q_BLOCK
    return pl.pallas_call(
        functools.partial(_nsa_prompt_body, n_slc=n_slc),
        grid=(B, S // Q_BLOCK),
        in_specs=[_col_window(Q_BLOCK, W, 4, True),
                  pl.BlockSpec((1, Q_BLOCK, gates.shape[-1]), lambda b, i: (b, i, 0)),
                  pl.BlockSpec((1, 128, KV_W), lambda b, i: (b, 0, 0)),
                  pl.BlockSpec((1, 128, KV_W), lambda b, i: (b, 0, 0))]
                 + [_col_window(S, KV_W, group, False) for group in (7, 8, 9, 10)]
                 + [pl.BlockSpec((128, S), lambda b, i: (0, 0))],
        out_specs=pl.BlockSpec((1, Q_BLOCK, W), lambda b, i: (b, i, 0)),
        out_shape=jax.ShapeDtypeStruct((B, S, W), F32),
        scratch_shapes=[pltpu.VMEM((rows, 1), F32), pltpu.VMEM((rows, 1), F32), pltpu.VMEM((rows, HEAD_DIM), F32)],
        compiler_params=pltpu.CompilerParams(dimension_semantics=("arbitrary", "arbitrary"),
                                             vmem_limit_bytes=VMEM_LIMIT),
        name="nsa_prompt",
    )(y_bf, gates, k_cmp, v_cmp, y_bf, y_bf, y_bf, y_bf, expand)


def _pad_cmp(c):
    B, n = c.shape[:2]
    return jnp.pad(c.reshape(B, n, KV_W), ((0, 0), (0, 128 - n), (0, 0))).astype(BF16)


def nsa_prompt(packed, gates, kc, vc, cmp_pos_k, cmp_phi_k, cmp_pos_v, cmp_phi_v):
    B, S = gates.shape[:2]
    k_cmp = _pad_cmp(compress(kc, cmp_pos_k, cmp_phi_k))
    v_cmp = _pad_cmp(compress(vc, cmp_pos_v, cmp_phi_v))
    return nsa_prompt_attend(packed[1], gates.reshape(B, S, 3 * NSA_HEADS), k_cmp, v_cmp)


def _bf_round(x):
    return x.astype(BF16).astype(F32)


def _fox_sample_body(pt_ref, q_ref, kn_ref, vn_ref, ck_ref, cn_ref, cq_ref, *refs, n_pages):
    k_refs, v_refs, o_ref = refs[:n_pages], refs[n_pages:2 * n_pages], refs[2 * n_pages]
    R = q_ref.shape[1]
    H = FOX_HEADS
    scale = HEAD_DIM ** -0.5
    q = q_ref[0].astype(BF16)
    cq = cq_ref[0]
    r_iota = lax.broadcasted_iota(jnp.int32, (R, 1), 0)
    width = PAGE_SIZE * H
    same_head = (lax.broadcasted_iota(jnp.int32, (1, width), 1) % H) == (r_iota % H)
    segs = []
    for p in range(n_pages):
        s = lax.dot_general(q, k_refs[p][0].astype(BF16), NT, preferred_element_type=F32) * scale
        segs.append(jnp.where(same_head, s + cq - ck_ref[0, :, p * width:(p + 1) * width], NEG))
    l_new = lax.broadcasted_iota(jnp.int32, (1, kn_ref.shape[1]), 1)
    new_ok = ((l_new % H) == (r_iota % H)) & ((l_new // H) <= (r_iota // H))
    s = lax.dot_general(q, kn_ref[0].astype(BF16), NT, preferred_element_type=F32) * scale
    segs.append(jnp.where(new_ok, s + cq - cn_ref[0], NEG))
    m = functools.reduce(jnp.maximum, [jnp.max(s, axis=-1, keepdims=True) for s in segs])
    es = [jnp.exp(s - m) for s in segs]
    inv = 1.0 / functools.reduce(jnp.add, [jnp.sum(e, axis=-1, keepdims=True) for e in es])
    acc = jnp.dot((es[n_pages] * inv).astype(BF16), vn_ref[0].astype(BF16), preferred_element_type=F32)
    for p in range(n_pages):
        acc = acc + jnp.dot((es[p] * inv).astype(BF16), v_refs[p][0].astype(BF16), preferred_element_type=F32)
    o_ref[0] = acc


def _page_spec(p, rows):
    return pl.BlockSpec((1, rows, HEAD_DIM), lambda b, pt, p=p: (pt[b, p], 0, 0))


def fox_sample_attend(q_th, k_new, v_new, ck, cn, cq, page_table, k_pool, v_pool):
    B, R, D = q_th.shape
    n_pages = page_table.shape[1]
    rows = PAGE_SIZE * FOX_HEADS
    per_b = lambda shape: pl.BlockSpec((1,) + shape, lambda b, pt: (b, 0, 0))
    grid_spec = pltpu.PrefetchScalarGridSpec(
        num_scalar_prefetch=1, grid=(B,),
        in_specs=[per_b((R, D)), per_b(k_new.shape[1:]), per_b(v_new.shape[1:]), per_b(ck.shape[1:]),
                  per_b(cn.shape[1:]), per_b((R, 1))]
                 + [_page_spec(p, rows) for p in range(n_pages)]
                 + [_page_spec(p, rows) for p in range(n_pages)],
        out_specs=per_b((R, D)))
    return pl.pallas_call(
        functools.partial(_fox_sample_body, n_pages=n_pages),
        grid_spec=grid_spec,
        out_shape=jax.ShapeDtypeStruct((B, R, D), F32),
        compiler_params=pltpu.CompilerParams(dimension_semantics=("arbitrary",), vmem_limit_bytes=VMEM_LIMIT),
        name="fox_sample",
    )(page_table, q_th, k_new, v_new, ck, cn, cq, *([k_pool] * n_pages), *([v_pool] * n_pages))


def _nsa_sample_body(pt_ref, q_ref, gt_ref, ksn_ref, vsn_ref, kwn_ref, vwn_ref, wk_ref, wv_ref,
                     posk_ref, posv_ref, phik_ref, phiv_ref, expand_ref, *refs, n_pages, T):
    kc_refs = refs[:n_pages]
    vc_refs = refs[n_pages:2 * n_pages]
    ks_refs = refs[2 * n_pages:3 * n_pages]
    vs_refs = refs[3 * n_pages:4 * n_pages]
    o_ref = refs[4 * n_pages]
    G = NSA_KV_HEADS
    past = n_pages * PAGE_SIZE
    win_buf = wk_ref.shape[1] // G
    scale = HEAD_DIM ** -0.5
    R = NSA_GROUP * T
    chunks = PAGE_SIZE // CMP_STRIDE

    def group_rows(ref, g, n):
        return ref[pl.ds(0, 1), pl.ds(g, n, stride=G), :][0]

    def compress_pages(page_refs, pos_ref, phi_ref, g):
        first, second = [], []
        for p in range(n_pages):
            x = group_rows(page_refs[p], g, PAGE_SIZE)
            first.append((x * pos_ref[0]).reshape(chunks, CMP_STRIDE, HEAD_DIM).sum(axis=1))
            second.append((x * pos_ref[1]).reshape(chunks, CMP_STRIDE, HEAD_DIM).sum(axis=1))
        first = jnp.concatenate(first, axis=0)
        second = jnp.concatenate(second, axis=0)
        pooled = first + pltpu.roll(second, first.shape[0] - 1, axis=0)
        return jnp.dot(pooled.astype(BF16), phi_ref[...], preferred_element_type=F32).astype(BF16)

    r_iota = lax.broadcasted_iota(jnp.int32, (R, 1), 0)
    t_of_r = r_iota % T
    tq = past + t_of_r
    n_iota = lax.broadcasted_iota(jnp.int32, (1, 128), 1)
    t_cmp = n_iota * CMP_STRIDE + (CMP_BLOCK - 1)
    dist_c = (tq - t_cmp).astype(F32)
    mask_c = t_cmp <= tq
    hs_r = lax.broadcasted_iota(jnp.int32, (128, R), 0)
    hs_c = lax.broadcasted_iota(jnp.int32, (128, R), 1)
    head_sum = ((hs_r < R) & (hs_c % T == hs_r % T)).astype(BF16)
    t_lane = past + lax.broadcasted_iota(jnp.int32, (1, 128), 1) % T
    n_slc = -(-(past + T) // SEL_BLOCK)
    n_slc_pad = -(-n_slc // 8) * 8
    new_blk = past // SEL_BLOCK
    kpos = lax.broadcasted_iota(jnp.int32, (1, past), 1)
    dist_s = (tq - kpos).astype(F32)
    tw = past - win_buf + lax.broadcasted_iota(jnp.int32, (1, win_buf), 1)
    dwin = tq - tw
    mask_w = (dwin >= 0) & (dwin < WINDOW) & (tw >= 0)
    dwin_f = dwin.astype(F32)

    def softmax_parts(s_past, mask_past, new_scores):
        s_past = jnp.where(mask_past, s_past, NEG)
        m = jnp.max(s_past, axis=-1, keepdims=True)
        for sj in new_scores:
            m = jnp.maximum(m, sj)
        e_past = jnp.where(mask_past, jnp.exp(s_past - m), 0.0)
        e_new = [jnp.where(sj > 0.5 * NEG, jnp.exp(sj - m), 0.0) for sj in new_scores]
        denom = jnp.sum(e_past, axis=-1, keepdims=True)
        for ej in e_new:
            denom = denom + ej
        return (e_past / denom).astype(BF16), [ej / denom for ej in e_new]

    def new_scores_of(g, kn_ref, qf, slope):
        out = []
        for j in range(T):
            kj = _bf_round(kn_ref[0, j * G + g:j * G + g + 1, :])
            out.append(jnp.sum(qf * kj, axis=-1, keepdims=True) * scale - slope * (t_of_r - j).astype(F32))
        return out

    def add_new_values(acc, g, p_new, vn_ref):
        for j in range(T):
            acc = acc + _bf_round(p_new[j]) * _bf_round(vn_ref[0, j * G + g:j * G + g + 1, :])
        return acc

    groups = range(G)
    rows = [slice(g * R, (g + 1) * R) for g in groups]
    slopes = []
    for g in groups:
        slope = jnp.zeros((R, 1), F32)
        for h in range(NSA_GROUP):
            slope = jnp.where(r_iota // T == h, 2.0 ** -(g * NSA_GROUP + h + 1), slope)
        slopes.append(slope)
    qs = [q_ref[0, rows[g], :].astype(BF16) for g in groups]
    qfs = [q.astype(F32) for q in qs]
    k_cmp = [compress_pages(kc_refs, posk_ref, phik_ref, g) for g in groups]
    v_cmp = [compress_pages(vc_refs, posv_ref, phiv_ref, g) for g in groups]

    s_cmp = [lax.dot_general(qs[g], k_cmp[g], NT, preferred_element_type=F32) * scale - slopes[g] * dist_c
             for g in groups]
    s_win = [lax.dot_general(qs[g], group_rows(wk_ref, g, win_buf).astype(BF16), NT,
                             preferred_element_type=F32) * scale - slopes[g] * dwin_f for g in groups]
    s_sel = [jnp.concatenate(
        [lax.dot_general(qs[g], group_rows(ks_refs[p], g, PAGE_SIZE).astype(BF16), NT, preferred_element_type=F32)
         for p in range(n_pages)], axis=1) * scale - slopes[g] * dist_s for g in groups]
    new_win = [new_scores_of(g, kwn_ref, qfs[g], slopes[g]) for g in groups]
    new_sel = [new_scores_of(g, ksn_ref, qfs[g], slopes[g]) for g in groups]

    p_cmp = [_masked_softmax_rows(s_cmp[g], mask_c) for g in groups]
    o_cmp = [jnp.dot(p_cmp[g].astype(BF16), v_cmp[g], preferred_element_type=F32) for g in groups]
    p_sum = [sum(jnp.dot(head_sum, x, preferred_element_type=F32) for x in _split3(p_cmp[g])) for g in groups]

    pw = [softmax_parts(s_win[g], mask_w, [jnp.where(t_of_r >= j, new_win[g][j], NEG) for j in range(T)])
          for g in groups]
    o_win = [add_new_values(jnp.dot(pw[g][0], group_rows(wv_ref, g, win_buf).astype(BF16), preferred_element_type=F32),
                            g, pw[g][1], vwn_ref) for g in groups]

    picked_all = _select_blocks(jnp.concatenate(p_sum, axis=0), jnp.concatenate([t_lane] * G, axis=1),
                                n_slc_pad, min(N_SEL, n_slc))
    picked = [picked_all[g * 128:g * 128 + R] for g in groups]

    sel_mask = [jnp.dot(picked[g].astype(BF16), expand_ref[...], preferred_element_type=F32) > 0.5 for g in groups]
    ps = [softmax_parts(s_sel[g], sel_mask[g],
                        [jnp.where((picked[g][:, new_blk:new_blk + 1] > 0.5) & (t_of_r >= j), new_sel[g][j], NEG)
                         for j in range(T)]) for g in groups]
    o_sel = []
    for g in groups:
        acc = jnp.zeros((R, HEAD_DIM), F32)
        for p in range(n_pages):
            acc = acc + jnp.dot(ps[g][0][:, p * PAGE_SIZE:(p + 1) * PAGE_SIZE],
                                group_rows(vs_refs[p], g, PAGE_SIZE).astype(BF16), preferred_element_type=F32)
        o_sel.append(add_new_values(acc, g, ps[g][1], vsn_ref))

    for g in groups:
        gts = gt_ref[0, rows[g], :]
        o_ref[0, rows[g], :] = gts[:, 0:1] * o_cmp[g] + gts[:, 1:2] * o_sel[g] + gts[:, 2:3] * o_win[g]


def nsa_sample_attend(nq_ht, gates_ht, ks_new, vs_new, kw_new, vw_new, win_k, win_v, pos_k2, pos_v2, phi_k, phi_v,
                      page_table, ck_pool, cv_pool, sk_pool, sv_pool):
    B, RT, D = nq_ht.shape
    T = RT // NSA_HEADS
    G = NSA_KV_HEADS
    n_pages = page_table.shape[1]
    past = n_pages * PAGE_SIZE
    expand = (jnp.arange(past)[None, :] // SEL_BLOCK == jnp.arange(128)[:, None]).astype(BF16)
    assert (n_pages * PAGE_SIZE + T) // CMP_STRIDE == n_pages * PAGE_SIZE // CMP_STRIDE and T <= SEL_BLOCK
    per_b = lambda shape: pl.BlockSpec((1,) + shape, lambda b, pt: (b, 0, 0))
    const = lambda shape: pl.BlockSpec(shape, lambda b, pt: (0,) * len(shape))
    grid_spec = pltpu.PrefetchScalarGridSpec(
        num_scalar_prefetch=1, grid=(B,),
        in_specs=[per_b((RT, D)), per_b((RT, 3))] + [per_b((T * G, D))] * 4 + [per_b(win_k.shape[1:])] * 2
                 + [const((2, PAGE_SIZE, D))] * 2 + [const((HEAD_DIM, HEAD_DIM))] * 2 + [const(expand.shape)]
                 + [_page_spec(p, PAGE_SIZE * G) for p in range(n_pages)] * 4,
        out_specs=per_b((RT, D)))
    pools = [ck_pool] * n_pages + [cv_pool] * n_pages + [sk_pool] * n_pages + [sv_pool] * n_pages
    return pl.pallas_call(
        functools.partial(_nsa_sample_body, n_pages=n_pages, T=T),
        grid_spec=grid_spec,
        out_shape=jax.ShapeDtypeStruct((B, RT, D), F32),
        compiler_params=pltpu.CompilerParams(dimension_semantics=("arbitrary",), vmem_limit_bytes=VMEM_LIMIT),
        name="nsa_sample",
    )(page_table, nq_ht, gates_ht, ks_new, vs_new, kw_new, vw_new, win_k, win_v, pos_k2, pos_v2, phi_k, phi_v,
      expand, *pools)


def _tile_pos(w_pos):
    halves = w_pos.reshape(2, CMP_STRIDE, HEAD_DIM)
    return jnp.tile(halves, (1, PAGE_SIZE // CMP_STRIDE, 1))


def mixers_sample(fq, fk, fv, lf, nq, gates, kc, vc, ks, vs, kw, vw, win_k, win_v, page_table,
                  fox_k_pool, fox_v_pool, fox_lf_pool, cmp_k_pool, cmp_v_pool, slc_k_pool, slc_v_pool,
                  cmp_pos_k, cmp_phi_k, cmp_pos_v, cmp_phi_v):
    B, T = fq.shape[:2]
    n_phys = fox_k_pool.shape[0]
    past = page_table.shape[1] * PAGE_SIZE
    H = FOX_HEADS
    assert T * H <= HEAD_DIM
    lf_past = fox_lf_pool[page_table].reshape(B, past, H)
    c = jnp.cumsum(jnp.concatenate([lf_past, lf], axis=1), axis=1)
    c_new = c[:, past:].reshape(B, 1, T * H)
    pad_rows = lambda a: jnp.pad(a.reshape(B, T * H, HEAD_DIM), ((0, 0), (0, HEAD_DIM - T * H), (0, 0)))
    fox_o = fox_sample_attend(fq.reshape(B, T * H, HEAD_DIM), pad_rows(fk), pad_rows(fv),
                              c[:, :past].reshape(B, 1, past * H),
                              jnp.pad(c_new, ((0, 0), (0, 0), (0, HEAD_DIM - T * H))), c_new.reshape(B, T * H, 1),
                              page_table, fox_k_pool.reshape(n_phys, PAGE_SIZE * H, HEAD_DIM),
                              fox_v_pool.reshape(n_phys, PAGE_SIZE * H, HEAD_DIM)).reshape(B, T, FOX_W)
    ht = lambda a: jnp.swapaxes(a, 1, 2).reshape(B, NSA_HEADS * T, a.shape[-1])
    flat = lambda a: a.reshape(a.shape[0], a.shape[1] * NSA_KV_HEADS, HEAD_DIM)
    nsa_ht = nsa_sample_attend(ht(nq), ht(gates), flat(ks), flat(vs), flat(kw), flat(vw), flat(win_k), flat(win_v),
                               _tile_pos(cmp_pos_k), _tile_pos(cmp_pos_v), cmp_phi_k.astype(BF16), cmp_phi_v.astype(BF16),
                               page_table, flat(cmp_k_pool), flat(cmp_v_pool), flat(slc_k_pool), flat(slc_v_pool))
    nsa_o = jnp.swapaxes(nsa_ht.reshape(B, NSA_HEADS, T, HEAD_DIM), 1, 2).reshape(B, T, NSA_W)
    return fox_o, nsa_o


def merge_mixers(x, fox_o, nsa_o, g_fox_out, g_nsa_out, w_out_bf):
    B, T, D = x.shape
    n = B * T
    y = project([(fox_o.reshape(n, FOX_W), g_fox_out, w_out_bf[:FOX_W]),
                 (nsa_o.reshape(n, NSA_W), g_nsa_out, w_out_bf[FOX_W:])], res=x.reshape(n, D))
    return y.reshape(B, T, D)


def memory_kv(mem, g, w_mkv_bf):
    B, M, D = mem.shape
    kv = project([(mem.reshape(B * M, D), g, w_mkv_bf)])
    return (kv[:, :MEM_W].reshape(B, M, MEM_HEADS, HEAD_DIM), kv[:, MEM_W:].reshape(B, M, MEM_HEADS, HEAD_DIM))


def _mem_attend_body(q_ref, k_ref, v_ref, o_ref):
    H = MEM_HEADS
    q = q_ref[0].astype(BF16)
    s = lax.dot_general(q, k_ref[0].astype(BF16), NT, preferred_element_type=F32) * HEAD_DIM ** -0.5
    r_iota = lax.broadcasted_iota(jnp.int32, (s.shape[0], 1), 0)
    c_iota = lax.broadcasted_iota(jnp.int32, (1, s.shape[1]), 1)
    s = jnp.where((c_iota % H) == (r_iota % H), s, NEG)
    e = jnp.exp(s - jnp.max(s, axis=-1, keepdims=True))
    p = e / jnp.sum(e, axis=-1, keepdims=True)
    o_ref[0] = jnp.dot(p.astype(BF16), v_ref[0].astype(BF16), preferred_element_type=F32)


def memory_attend(x, g, w_mq_bf, w_mo_bf, mk, mv):
    B, T, D = x.shape
    M, H = mk.shape[1], MEM_HEADS
    x2 = x.reshape(B * T, D)
    q = project([(x2, g, w_mq_bf)]).reshape(B, T * H, HEAD_DIM)
    rows = min(MEM_ROW_TILE, T * H)
    kv_spec = pl.BlockSpec((1, M * H, HEAD_DIM), lambda b, i: (b, 0, 0))
    o = pl.pallas_call(
        _mem_attend_body,
        grid=(B, T * H // rows),
        in_specs=[pl.BlockSpec((1, rows, HEAD_DIM), lambda b, i: (b, i, 0)), kv_spec, kv_spec],
        out_specs=pl.BlockSpec((1, rows, HEAD_DIM), lambda b, i: (b, i, 0)),
        out_shape=jax.ShapeDtypeStruct((B, T * H, HEAD_DIM), F32),
        compiler_params=pltpu.CompilerParams(dimension_semantics=("arbitrary", "arbitrary"),
                                             vmem_limit_bytes=VMEM_LIMIT),
        name="mem_attend",
    )(q, mk.reshape(B, M * H, HEAD_DIM), mv.reshape(B, M * H, HEAD_DIM))
    return project([(o.reshape(B * T, MEM_W), None, w_mo_bf)], res=x2).reshape(B, T, D)


def _top16_rows(s):
    iota = lax.broadcasted_iota(jnp.int32, s.shape, 0)
    vals = []
    for _ in range(PEER_TOPK):
        m = jnp.max(s, axis=0, keepdims=True)
        idx = jnp.min(jnp.where(s == m, iota, PEER_KEYS), axis=0, keepdims=True)
        s = jnp.where(iota == idx, NEG_INF, s)
        vals.append(m)
    return s == NEG_INF, jnp.concatenate(vals, axis=0)


def _peer_scores_body(x_ref, g_ref, wq_ref, k1_ref, k2_ref, xn_ref, a_ref, b_ref, tau_ref):
    h = pl.program_id(1)

    @pl.when(h == 0)
    def _():
        xf = x_ref[...]
        y = xf * lax.rsqrt(jnp.mean(xf * xf, axis=-1, keepdims=True) + EPS) * g_ref[...]
        xn_ref[...] = y.astype(BF16)

    qT = lax.dot_general(wq_ref[...], xn_ref[...], (((1,), (1,)), ((), ())), preferred_element_type=F32)
    half = qT.shape[0] // 2
    s1 = jnp.dot(k1_ref[...], qT[:half].astype(BF16), preferred_element_type=F32)
    s2 = jnp.dot(k2_ref[...], qT[half:].astype(BF16), preferred_element_type=F32)
    mem1, v1 = _top16_rows(s1)
    mem2, v2 = _top16_rows(s2)
    pieces = [v1[0:1] + v2]
    for r in range(1, 8):
        pieces.append(v1[r:r + 1] + v2[0:8])
    pieces.append(v1[8:16] + v2[0:1])
    c = jnp.concatenate(pieces, axis=0)
    row = lax.broadcasted_iota(jnp.int32, c.shape, 0)
    rest = c
    for _ in range(PEER_TOPK):
        c16 = jnp.max(rest, axis=0, keepdims=True)
        first = jnp.min(jnp.where(rest == c16, row, c.shape[0]), axis=0, keepdims=True)
        rest = jnp.where(row == first, NEG_INF, rest)
    c17 = jnp.max(rest, axis=0, keepdims=True)
    z = jnp.sum(jnp.where(c >= c16, jnp.exp(c - c[0:1]), 0.0), axis=0, keepdims=True)
    shift = v1[0:1] + jnp.log(z)
    log2e = math.log2(math.e)
    a_ref[0] = jnp.where(mem1, (s1 - shift) * log2e, NEG_INF)
    b_ref[0] = jnp.where(mem2, (s2 - v2[0:1]) * log2e, NEG_INF)
    tau_ref[0] = (0.5 * (c16 + c17) - shift - v2[0:1]) * log2e


def peer_scores(x2, g, wqT_bf, k1_bf, k2_bf, tt):
    n, d = x2.shape
    H = PEER_HEADS
    qd = wqT_bf.shape[0] // H
    return pl.pallas_call(
        _peer_scores_body,
        grid=(n // tt, H),
        in_specs=[pl.BlockSpec((tt, d), lambda i, h: (i, 0)),
                  pl.BlockSpec((1, d), lambda i, h: (0, 0)),
                  pl.BlockSpec((qd, d), lambda i, h: (h, 0)),
                  pl.BlockSpec((PEER_KEYS, qd // 2), lambda i, h: (0, 0)),
                  pl.BlockSpec((PEER_KEYS, qd // 2), lambda i, h: (0, 0))],
        out_specs=[pl.BlockSpec((tt, d), lambda i, h: (i, 0))]
                  + [pl.BlockSpec((1, PEER_KEYS, tt), lambda i, h: (h, 0, i))] * 2
                  + [pl.BlockSpec((1, 1, tt), lambda i, h: (h, 0, i))],
        out_shape=[jax.ShapeDtypeStruct((n, d), BF16)]
                  + [jax.ShapeDtypeStruct((H, PEER_KEYS, n), F32)] * 2
                  + [jax.ShapeDtypeStruct((H, 1, n), F32)],
        compiler_params=pltpu.CompilerParams(dimension_semantics=("arbitrary", "arbitrary"),
                                             vmem_limit_bytes=VMEM_LIMIT),
        name="peer_scores",
    )(x2, g.reshape(1, d), wqT_bf, k1_bf, k2_bf)


def _gelu_exact(x):
    return 0.5 * x * (1.0 + lax.erf(x * (2.0 ** -0.5)))


def _peer_dense_body(x_ref, gf_ref, xn_ref, u_ref, vt_ref, a_ref, b_ref, tau_ref, o_ref, acc_ref, *sub_refs,
                     te, final_norm):
    j = pl.program_id(1)
    h_refs, p_refs = sub_refs[:PEER_SUBTILES], sub_refs[PEER_SUBTILES:]

    @pl.when(j == 0)
    def _():
        acc_ref[...] = jnp.zeros(acc_ref.shape, F32)

    sub_rows = te // PEER_SUBTILES
    na_sub = sub_rows // PEER_KEYS
    lane_tile = PEER_KEYS
    for k in range(PEER_SUBTILES):
        h_refs[k][...] = lax.dot_general(u_ref[k * sub_rows:(k + 1) * sub_rows, :], xn_ref[...], NT,
                                         preferred_element_type=F32)
    for k in range(PEER_SUBTILES):
        for al in range(na_sub):
            a = (j * PEER_SUBTILES + k) * na_sub + al
            rows = slice(al * PEER_KEYS, (al + 1) * PEER_KEYS)
            a_rows = [a_ref[h, pl.ds(a, 1), :] for h in range(PEER_HEADS)]
            for c in range(h_refs[k].shape[1] // lane_tile):
                lanes = slice(c * lane_tile, (c + 1) * lane_tile)
                w = jnp.zeros((PEER_KEYS, lane_tile), F32)
                for h in range(PEER_HEADS):
                    t = a_rows[h][:, lanes] + b_ref[h, :, lanes]
                    w = w + jnp.where(t >= tau_ref[h, :, lanes], jnp.exp2(t), 0.0)
                p_refs[k][rows, lanes] = (w * _gelu_exact(h_refs[k][rows, lanes])).astype(BF16)
        acc_ref[...] += jnp.dot(vt_ref[0, :, k * sub_rows:(k + 1) * sub_rows], p_refs[k][...],
                                preferred_element_type=F32)

    @pl.when(j == pl.num_programs(1) - 1)
    def _():
        y = x_ref[...] + acc_ref[...].T
        if final_norm:
            y = y * lax.rsqrt(jnp.mean(y * y, axis=-1, keepdims=True) + EPS) * gf_ref[...]
        o_ref[...] = y


def expert_tiles_transposed(v):
    ne, d = v.shape
    return jnp.swapaxes(v.reshape(ne // PEER_EXPERT_TILE, PEER_EXPERT_TILE, d), 1, 2).astype(BF16)


def peer_dense(x2, g_final, xn_bf, u_bf, vT_bf, a, b, tau, tt, te, final_norm):
    n, d = xn_bf.shape
    ne = u_bf.shape[0]
    assert vT_bf.shape == (ne // te, d, te)
    H = PEER_HEADS
    return pl.pallas_call(
        functools.partial(_peer_dense_body, te=te, final_norm=final_norm),
        grid=(n // tt, ne // te),
        in_specs=[pl.BlockSpec((tt, d), lambda i, j: (i, 0), pipeline_mode=pl.Buffered(1)),
                  pl.BlockSpec((1, d), lambda i, j: (0, 0)),
                  pl.BlockSpec((tt, d), lambda i, j: (i, 0)),
                  pl.BlockSpec((te, d), lambda i, j: (j, 0)),
                  pl.BlockSpec((1, d, te), lambda i, j: (j, 0, 0))]
                 + [pl.BlockSpec((H, PEER_KEYS, tt), lambda i, j: (0, 0, i), pipeline_mode=pl.Buffered(1))] * 2
                 + [pl.BlockSpec((H, 1, tt), lambda i, j: (0, 0, i))],
        out_specs=pl.BlockSpec((tt, d), lambda i, j: (i, 0)),
        out_shape=jax.ShapeDtypeStruct((n, d), F32),
        scratch_shapes=[pltpu.VMEM((d, tt), F32)]
                       + [pltpu.VMEM((te // PEER_SUBTILES, tt), F32)] * PEER_SUBTILES
                       + [pltpu.VMEM((te // PEER_SUBTILES, tt), BF16)] * PEER_SUBTILES,
        compiler_params=pltpu.CompilerParams(dimension_semantics=("arbitrary", "arbitrary"),
                                             vmem_limit_bytes=VMEM_LIMIT),
        name="peer_dense",
    )(x2, g_final.reshape(1, d), xn_bf, u_bf, vT_bf, a, b, tau)


def peer_ffn(x, g, wqT_bf, k1_bf, k2_bf, u_bf, vT_bf, g_final, final_norm):
    B, T, D = x.shape
    x2 = x.reshape(B * T, D)
    tt = min(PEER_TOKEN_TILE, B * T)
    assert (B * T) % tt == 0
    xn_bf, a, b, tau = peer_scores(x2, g, wqT_bf, k1_bf, k2_bf, tt)
    out = peer_dense(x2, g_final, xn_bf, u_bf, vT_bf, a, b, tau, tt, PEER_EXPERT_TILE, final_norm)
    return out.reshape(B, T, D)


def kernel(x_prompt, x_sample, mem_prompt, cache_fox_k, cache_fox_v, cache_fox_logf,
           cache_cmp_k, cache_cmp_v, cache_slc_k, cache_slc_v, state_win_k, state_win_v,
           cache_mem_k, cache_mem_v, page_table,
           g_mix, w_in, b_fox_f, b_nsa_gate, cmp_pos_k, cmp_phi_k, cmp_pos_v, cmp_phi_v,
           g_fox_out, g_nsa_out, w_out, g_mem_q, g_mem_kv, w_mq, w_mk, w_mv, w_mo,
           g_peer, w_pq, peer_subkey_1, peer_subkey_2, peer_u, peer_v, g_final):
    depth = g_mix.shape[0]
    win_buf = min(WINDOW, PAST_LEN)
    names_p = ('fox_k', 'fox_v', 'fox_logf', 'cmp_k', 'cmp_v', 'slc_k', 'slc_v', 'win_k', 'win_v', 'mem_k', 'mem_v')
    names_s = ('fox_k', 'fox_v', 'fox_logf', 'cmp_k', 'cmp_v', 'slc_k', 'slc_v', 'win_k', 'win_v')
    sp = {n: [] for n in names_p}
    ss = {n: [] for n in names_s}
    xp, xs = x_prompt, x_sample
    for l in range(depth):
        w_in_bf = regroup_w_in(w_in[l])
        w_out_bf, w_mq_bf, w_mo_bf = w_out[l].astype(BF16), w_mq[l].astype(BF16), w_mo[l].astype(BF16)
        w_mkv_bf = jnp.concatenate([w_mk[l], w_mv[l]], axis=1).astype(BF16)
        peer_w = (w_pq[l].T.astype(BF16), peer_subkey_1[l].astype(BF16), peer_subkey_2[l].astype(BF16),
                  peer_u[l].astype(BF16), expert_tiles_transposed(peer_v[l]))

        (fq, fk, fv, lf, nq, kc, vc, ks, vs, kw, vw, gt), packed = project_mixers(
            xp, g_mix[l], w_in_bf, b_fox_f[l], b_nsa_gate[l], bf16_copy=True)
        fox_o = fox_prompt(packed, lf)
        nsa_o = nsa_prompt(packed, gt, kc, vc, cmp_pos_k[l], cmp_phi_k[l], cmp_pos_v[l], cmp_phi_v[l])
        xp = merge_mixers(xp, fox_o, nsa_o, g_fox_out[l], g_nsa_out[l], w_out_bf)
        mk, mv = memory_kv(mem_prompt, g_mem_kv[l], w_mkv_bf)
        xp = memory_attend(xp, g_mem_q[l], w_mq_bf, w_mo_bf, mk, mv)
        last = l == depth - 1
        xp = peer_ffn(xp, g_peer[l], *peer_w, g_final, last)
        for name, val in zip(names_p, (fk, fv, lf, kc, vc, ks, vs, last_rows(kw, win_buf), last_rows(vw, win_buf), mk, mv)):
            sp[name].append(val)

        (fq, fk, fv, lf, nq, kc, vc, ks, vs, kw, vw, gt), _ = project_mixers(
            xs, g_mix[l], w_in_bf, b_fox_f[l], b_nsa_gate[l])
        kw_all = jnp.concatenate([state_win_k[l].astype(kw.dtype), kw], axis=1)
        vw_all = jnp.concatenate([state_win_v[l].astype(vw.dtype), vw], axis=1)
        fox_o, nsa_o = mixers_sample(fq, fk, fv, lf, nq, gt, kc, vc, ks, vs, kw, vw,
                                     state_win_k[l], state_win_v[l], page_table,
                                     cache_fox_k[l], cache_fox_v[l], cache_fox_logf[l],
                                     cache_cmp_k[l], cache_cmp_v[l], cache_slc_k[l], cache_slc_v[l],
                                     cmp_pos_k[l], cmp_phi_k[l], cmp_pos_v[l], cmp_phi_v[l])
        xs = merge_mixers(xs, fox_o, nsa_o, g_fox_out[l], g_nsa_out[l], w_out_bf)
        xs = memory_attend(xs, g_mem_q[l], w_mq_bf, w_mo_bf, cache_mem_k[l], cache_mem_v[l])
        xs = peer_ffn(xs, g_peer[l], *peer_w, g_final, last)
        for name, val in zip(names_s, (fk, fv, lf, kc, vc, ks, vs, kw_all[:, -win_buf:], vw_all[:, -win_buf:])):
            ss[name].append(val)

    y_prompt, y_sample = (xp, xs) if depth else (rmsnorm_pallas(xp, g_final), rmsnorm_pallas(xs, g_final))
    sp = {n: jnp.stack(v) for n, v in sp.items()}
    ss = {n: jnp.stack(v) for n, v in ss.items()}
    return (y_prompt, y_sample,
            sp['fox_k'], sp['fox_v'], sp['fox_logf'], sp['cmp_k'], sp['cmp_v'], sp['slc_k'], sp['slc_v'],
            sp['win_k'], sp['win_v'], sp['mem_k'], sp['mem_v'],
            ss['fox_k'], ss['fox_v'], ss['fox_logf'], ss['cmp_k'], ss['cmp_v'], ss['slc_k'], ss['slc_v'],
            ss['win_k'], ss['win_v'])
```
